```python
import jax
import jax.numpy as jnp
from jax import lax
import numpy as np

D_MODEL = 1024
BATCH = 2
SEQ = 8192
DEPTH = 4
DEC_BATCH = 128
DEC_SEQ = 1
PAST_LEN = 2048
PAGE_SIZE = 128

N_MEM = 256
NORM_EPS = 1e-6
D_FF = 2816
RW_HEAD = 64
RW_HEADS = D_MODEL // RW_HEAD
RW_DECAY_LORA = 64
RW_A_LORA = 64
RW_V_LORA = 32
RW_G_LORA = 128
RW_LN_EPS = RW_HEAD * 1e-5
SWA_WINDOWS = (128, 512, 2048)
SWA_DILATIONS = (1, 4, 16)
N_GROUPS = len(SWA_WINDOWS)
SWA_HEADS = 8
SWA_HEAD_DIM = 64
SWA_WIDTH = SWA_HEADS * SWA_HEAD_DIM
SWA_BLOCK = 128
XA_HEADS = 4
XA_HEAD_DIM = 128
XA_WIDTH = XA_HEADS * XA_HEAD_DIM
N_RWKV = (DEPTH + 1) // 2
N_ATT = DEPTH // 2

kernel_name = 'hybrid_rwkv7_dilated_swa_step'


def rms_norm(x, g):
    xf = x.astype(jnp.float32)
    y = xf * lax.rsqrt(jnp.mean(xf * xf, axis=-1, keepdims=True) + NORM_EPS)
    return (y * g.astype(jnp.float32)).astype(x.dtype)


def swiglu(x, w_in, w_out):
    gate, up = jnp.split(x @ w_in, 2, axis=-1)
    return (jax.nn.silu(gate) * up) @ w_out


def alibi_slopes(n):
    return 2.0 ** (-8.0 * jnp.arange(1, n + 1, dtype=jnp.float32) / n)


def memory_kv(mem, g, w_kv):
    b, m, _ = mem.shape
    return (rms_norm(mem, g) @ w_kv).reshape(b, m, 2, XA_HEADS, XA_HEAD_DIM)


def cross_attend(xn, kv, w_q, w_o):
    b, t, _ = xn.shape
    q = (xn @ w_q).reshape(b, t, XA_HEADS, XA_HEAD_DIM)
    s = jnp.einsum('bqhd,bkhd->bhqk', q, kv[:, :, 0]).astype(jnp.float32) * (XA_HEAD_DIM ** -0.5)
    p = jax.nn.softmax(s, axis=-1).astype(xn.dtype)
    o = jnp.einsum('bhqk,bkhd->bqhd', p, kv[:, :, 1])
    return o.reshape(b, t, XA_WIDTH) @ w_o


def wkv7_scan(s0, r, w, k, v, a, b):
    def step(s, inp):
        r_t, w_t, k_t, v_t, a_t, b_t = inp
        sa = jnp.einsum('bhij,bhj->bhi', s, a_t)
        s = s * w_t[:, :, None, :] + sa[..., None] * b_t[:, :, None, :] + v_t[..., None] * k_t[:, :, None, :]
        return s, jnp.einsum('bhij,bhj->bhi', s, r_t)
    xs = tuple(jnp.swapaxes(z, 0, 1) for z in (r, w, k, v, a, b))
    s, ys = lax.scan(step, s0.astype(jnp.float32), xs)
    return s, jnp.swapaxes(ys, 0, 1)


def rwkv7_time_mix(xn, shift_row, s0, v_first, mix, w_rkv, w_o, w0, w1, w2, a0, a1, a2,
                   g1, g2, k_k, k_a, r_k, ln_w, ln_b, vres):
    b, t, d = xn.shape
    x_prev = jnp.concatenate([shift_row[:, None].astype(xn.dtype), xn[:, :-1]], axis=1)
    xx = x_prev - xn
    xr, xw, xk, xv, xa, xg = (xn + xx * mix[j] for j in range(6))
    r = xr @ w_rkv[0]
    k = xk @ w_rkv[1]
    v = xv @ w_rkv[2]
    w_log = -jax.nn.softplus(-(w0 + jnp.tanh(xw @ w1) @ w2)) - 0.5
    decay = jnp.exp(-jnp.exp(w_log.astype(jnp.float32)))
    a = jax.nn.sigmoid(a0 + (xa @ a1) @ a2)
    g = jax.nn.sigmoid(xg @ g1) @ g2
    v_own = v
    if vres is not None:
        v0, v1, v2 = vres
        v = v + (v_first - v) * jax.nn.sigmoid(v0 + (xv @ v1) @ v2)
    hs = lambda z: z.reshape(b, t, RW_HEADS, RW_HEAD).astype(jnp.float32)
    kk = hs(k * k_k)
    kk = kk * lax.rsqrt(jnp.maximum(jnp.sum(kk * kk, axis=-1, keepdims=True), 1e-24))
    k = k * (1 + (a - 1) * k_a)
    rh, kh, vh, ah = hs(r), hs(k), hs(v), hs(a)
    s_new, y = wkv7_scan(s0, rh, hs(decay), kh, vh, -kk, kk * ah)
    mu = jnp.mean(y, axis=-1, keepdims=True)
    var = jnp.mean((y - mu) ** 2, axis=-1, keepdims=True)
    yn = ((y - mu) * lax.rsqrt(var + RW_LN_EPS)).reshape(b, t, d) * ln_w + ln_b
    bonus = jnp.sum(rh * kh * r_k, axis=-1, keepdims=True) * vh
    y = (yn + bonus.reshape(b, t, d)).astype(xn.dtype)
    return (y * g) @ w_o, v_own, s_new.astype(xn.dtype), xn[:, -1]


def dilated_band_attention(q, k, v, dil, units, slopes):
    b, s_len, h, dh = q.shape
    L = s_len // dil
    nb = -(-L // SWA_BLOCK)
    lp = nb * SWA_BLOCK

    def blocks(z):
        z = z.reshape(b, L, dil, h, dh)
        z = jnp.pad(z, ((0, 0), (0, lp - L), (0, 0), (0, 0), (0, 0)))
        return z.reshape(b, nb, SWA_BLOCK, dil, h, dh)

    def with_prev(z):
        prev = jnp.concatenate([jnp.zeros_like(z[:, :1]), z[:, :-1]], axis=1)
        return jnp.concatenate([prev, z], axis=2)

    qb = blocks(q)
    kk = with_prev(blocks(k))
    vv = with_prev(blocks(v))
    s = jnp.einsum('bnqrhd,bnkrhd->bnrhqk', qb, kk).astype(jnp.float32) * (dh ** -0.5)
    ki = jnp.arange(2 * SWA_BLOCK)[None, :]
    delta = jnp.arange(SWA_BLOCK)[:, None] + SWA_BLOCK - ki
    band = (delta >= 0) & (delta <= units)
    key_ok = (jnp.arange(nb)[:, None] * SWA_BLOCK - SWA_BLOCK + ki) >= 0
    valid = band[None] & key_ok[:, None, :]
    bias = -slopes[:, None, None] * (dil * delta).astype(jnp.float32)[None]
    s = jnp.where(valid[None, :, None, None], s + bias, -jnp.inf)
    lse = jax.nn.logsumexp(s, axis=-1)
    p = jnp.exp(s - lse[..., None]).astype(v.dtype)
    o = jnp.einsum('bnrhqk,bnkrhd->bnqrhd', p, vv).reshape(b, lp, dil, h, dh)[:, :L]
    lse = jnp.transpose(lse, (0, 1, 4, 2, 3)).reshape(b, lp, dil, h)[:, :L]
    return o.reshape(b, s_len, h, dh), lse.reshape(b, s_len, h)


def dilated_gather_attention(q, k_all, v_all, n_past, dil, units, slopes):
    b, t, h, dh = q.shape
    j = jnp.arange(units + 1)
    idx = n_past + jnp.arange(t)[:, None] - j[None, :] * dil
    ok = idx >= 0
    idx = jnp.maximum(idx, 0)
    kg = k_all[:, idx]
    vg = v_all[:, idx]
    s = jnp.einsum('bqhd,bqjhd->bhqj', q, kg).astype(jnp.float32) * (dh ** -0.5)
    bias = -slopes[:, None, None] * (j * dil).astype(jnp.float32)[None, None, :]
    s = jnp.where(ok[None, None], s + bias, -jnp.inf)
    lse = jax.nn.logsumexp(s, axis=-1)
    p = jnp.exp(s - lse[..., None]).astype(v_all.dtype)
    o = jnp.einsum('bhqj,bqjhd->bqhd', p, vg)
    return o, jnp.transpose(lse, (0, 2, 1))


def merge_groups(outs, lses, w_o):
    o = jnp.stack(outs, axis=2).astype(jnp.float32)
    alpha = jax.nn.softmax(jnp.stack(lses, axis=2), axis=2)
    y = jnp.sum(alpha[..., None] * o, axis=2).astype(outs[0].dtype)
    b, t = y.shape[:2]
    return y.reshape(b, t, SWA_WIDTH) @ w_o


def run_layers(x, mem_kv, wkv0, shift0, swa_bufs, W):
    bsz, t = x.shape[0], x.shape[1]
    slopes = alibi_slopes(N_GROUPS * SWA_HEADS)
    new_wkv, new_shift = [], []
    new_swa = [[] for _ in range(N_GROUPS)]
    v_first = None
    for l in range(DEPTH):
        ng = W['norm_g'][l]
        x = x + 0.5 * swiglu(rms_norm(x, ng[0]), W['ffn_w_in'][l, 0], W['ffn_w_out'][l, 0])
        xn = rms_norm(x, ng[1])
        i = l // 2
        if l % 2 == 0:
            vres = None if i == 0 else (W['rw_v0'][i - 1], W['rw_v1'][i - 1], W['rw_v2'][i - 1])
            y, v_own, s_new, sh_new = rwkv7_time_mix(
                xn, shift0[i], wkv0[i], v_first, W['rw_mix'][i], W['rw_w_rkv'][i], W['rw_w_o'][i],
                W['rw_w0'][i], W['rw_w1'][i], W['rw_w2'][i], W['rw_a0'][i], W['rw_a1'][i], W['rw_a2'][i],
                W['rw_g1'][i], W['rw_g2'][i], W['rw_k_k'][i], W['rw_k_a'][i], W['rw_r_k'][i],
                W['rw_ln_w'][i], W['rw_ln_b'][i], vres)
            if i == 0:
                v_first = v_own
            new_wkv.append(s_new)
            new_shift.append(sh_new)
        else:
            qkv = (xn @ W['swa_w_qkv'][i]).reshape(bsz, t, N_GROUPS, 3, SWA_HEADS, SWA_HEAD_DIM)
            outs, lses = [], []
            for g in range(N_GROUPS):
                q, k, v = qkv[:, :, g, 0], qkv[:, :, g, 1], qkv[:, :, g, 2]
                dil = SWA_DILATIONS[g]
                units = SWA_WINDOWS[g] // dil
                sl = slopes[g * SWA_HEADS:(g + 1) * SWA_HEADS]
                if swa_bufs is None:
                    o, lse = dilated_band_attention(q, k, v, dil, units, sl)
                    keep = min(SWA_WINDOWS[g], t)
                    new_swa[g].append(jnp.stack([k[:, t - keep:], v[:, t - keep:]], axis=2))
                else:
                    buf = swa_bufs[g][i]
                    n_past = buf.shape[1]
                    k_all = jnp.concatenate([buf[:, :, 0], k], axis=1)
                    v_all = jnp.concatenate([buf[:, :, 1], v], axis=1)
                    o, lse = dilated_gather_attention(q, k_all, v_all, n_past, dil, units, sl)
                    start = n_past + t - min(SWA_WINDOWS[g], n_past + t)
                    new_swa[g].append(jnp.stack([k_all[:, start:], v_all[:, start:]], axis=2))
                outs.append(o)
                lses.append(lse)
            y = merge_groups(outs, lses, W['swa_w_o'][i])
        x = x + y
        x = x + cross_attend(rms_norm(x, ng[2]), mem_kv[l], W['xa_w_q'][l], W['xa_w_o'][l])
        x = x + 0.5 * swiglu(rms_norm(x, ng[3]), W['ffn_w_in'][l, 1], W['ffn_w_out'][l, 1])
    swa_out = tuple(jnp.stack(z) for z in new_swa)
    return rms_norm(x, W['final_norm_g']), jnp.stack(new_wkv), jnp.stack(new_shift), swa_out


def setup_inputs(seed: int = 0) -> dict:
    key = jax.random.key(seed)
    ks = jax.random.split(key, 48)
    counter = [0]

    def nk():
        counter[0] += 1
        return ks[counter[0] - 1]

    def nrm(shape, scale=1.0):
        return jax.random.normal(nk(), shape, jnp.float32) * scale

    def gain(shape):
        return 1.0 + nrm(shape, 0.05)

    D = D_MODEL
    inp = {}
    inp['x_prompt'] = nrm((BATCH, SEQ, D))
    inp['x_sample'] = nrm((DEC_BATCH, DEC_SEQ, D))
    inp['cache_mem_kv'] = nrm((DEPTH, DEC_BATCH, N_MEM, 2, XA_HEADS, XA_HEAD_DIM))
    inp['state_rwkv_wkv'] = nrm((N_RWKV, DEC_BATCH, RW_HEADS, RW_HEAD, RW_HEAD), 0.5)
    inp['state_rwkv_shift'] = nrm((N_RWKV, DEC_BATCH, D))
    for g in range(N_GROUPS):
        inp['cache_swa_kv_g%d' % g] = nrm((N_ATT, DEC_BATCH, min(SWA_WINDOWS[g], PAST_LEN), 2, SWA_HEADS, SWA_HEAD_DIM))
    inp['mem_prompt'] = nrm((BATCH, N_MEM, D))
    inp['norm_g'] = gain((DEPTH, 4, D))
    inp['mem_norm_g'] = gain((DEPTH, D))
    inp['final_norm_g'] = gain((D,))
    inp['ffn_w_in'] = nrm((DEPTH, 2, D, 2 * D_FF), D ** -0.5)
    inp['ffn_w_out'] = nrm((DEPTH, 2, D_FF, D), D_FF ** -0.5)
    inp['rw_mix'] = jax.random.uniform(nk(), (N_RWKV, 6, D), jnp.float32)
    inp['rw_w_rkv'] = nrm((N_RWKV, 3, D, D), D ** -0.5)
    inp['rw_w_o'] = nrm((N_RWKV, D, D), D ** -0.5)
    inp['rw_w0'] = nrm((N_RWKV, D), 0.5) - 1.0
    inp['rw_w1'] = nrm((N_RWKV, D, RW_DECAY_LORA), D ** -0.5)
    inp['rw_w2'] = nrm((N_RWKV, RW_DECAY_LORA, D), 0.5 * RW_DECAY_LORA ** -0.5)
    inp['rw_a0'] = nrm((N_RWKV, D), 0.1)
    inp['rw_a1'] = nrm((N_RWKV, D, RW_A_LORA), D ** -0.5)
    inp['rw_a2'] = nrm((N_RWKV, RW_A_LORA, D), 0.5 * RW_A_LORA ** -0.5)
    inp['rw_v0'] = nrm((N_RWKV - 1, D), 0.1)
    inp['rw_v1'] = nrm((N_RWKV - 1, D, RW_V_LORA), D ** -0.5)
    inp['rw_v2'] = nrm((N_RWKV - 1, RW_V_LORA, D), 0.5 * RW_V_LORA ** -0.5)
    inp['rw_g1'] = nrm((N_RWKV, D, RW_G_LORA), D ** -0.5)
    inp['rw_g2'] = nrm((N_RWKV, RW_G_LORA, D), RW_G_LORA ** -0.5)
    inp['rw_k_k'] = 0.85 + nrm((N_RWKV, D), 0.05)
    inp['rw_k_a'] = gain((N_RWKV, D))
    inp['rw_r_k'] = nrm((N_RWKV, RW_HEADS, RW_HEAD), 0.1)
    inp['rw_ln_w'] = gain((N_RWKV, D))
    inp['rw_ln_b'] = nrm((N_RWKV, D), 0.02)
    inp['swa_w_qkv'] = nrm((N_ATT, D, N_GROUPS * 3 * SWA_WIDTH), D ** -0.5)
    inp['swa_w_o'] = nrm((N_ATT, SWA_WIDTH, D), SWA_WIDTH ** -0.5)
    inp['xa_w_q'] = nrm((DEPTH, D, XA_WIDTH), D ** -0.5)
    inp['xa_w_kv'] = nrm((DEPTH, D, 2 * XA_WIDTH), D ** -0.5)
    inp['xa_w_o'] = nrm((DEPTH, XA_WIDTH, D), XA_WIDTH ** -0.5)
    return inp


def reference(x_prompt, x_sample, cache_mem_kv, state_rwkv_wkv, state_rwkv_shift,
              cache_swa_kv_g0, cache_swa_kv_g1, cache_swa_kv_g2, mem_prompt,
              norm_g, mem_norm_g, final_norm_g, ffn_w_in, ffn_w_out,
              rw_mix, rw_w_rkv, rw_w_o, rw_w0, rw_w1, rw_w2, rw_a0, rw_a1, rw_a2,
              rw_v0, rw_v1, rw_v2, rw_g1, rw_g2, rw_k_k, rw_k_a, rw_r_k, rw_ln_w, rw_ln_b,
              swa_w_qkv, swa_w_o, xa_w_q, xa_w_kv, xa_w_o):
    W = dict(norm_g=norm_g, final_norm_g=final_norm_g, ffn_w_in=ffn_w_in, ffn_w_out=ffn_w_out,
             rw_mix=rw_mix, rw_w_rkv=rw_w_rkv, rw_w_o=rw_w_o, rw_w0=rw_w0, rw_w1=rw_w1, rw_w2=rw_w2,
             rw_a0=rw_a0, rw_a1=rw_a1, rw_a2=rw_a2, rw_v0=rw_v0, rw_v1=rw_v1, rw_v2=rw_v2,
             rw_g1=rw_g1, rw_g2=rw_g2, rw_k_k=rw_k_k, rw_k_a=rw_k_a, rw_r_k=rw_r_k,
             rw_ln_w=rw_ln_w, rw_ln_b=rw_ln_b, swa_w_qkv=swa_w_qkv, swa_w_o=swa_w_o,
             xa_w_q=xa_w_q, xa_w_o=xa_w_o)
    mem_kv_prompt = jnp.stack([memory_kv(mem_prompt, mem_norm_g[l], xa_w_kv[l]) for l in range(DEPTH)])
    b = x_prompt.shape[0]
    wkv_zero = jnp.zeros((N_RWKV, b, RW_HEADS, RW_HEAD, RW_HEAD), x_prompt.dtype)
    shift_zero = jnp.zeros((N_RWKV, b, D_MODEL), x_prompt.dtype)
    y_prompt, wkv_p, shift_p, swa_p = run_layers(x_prompt, mem_kv_prompt, wkv_zero, shift_zero, None, W)
    swa_bufs = (cache_swa_kv_g0, cache_swa_kv_g1, cache_swa_kv_g2)
    y_sample, wkv_s, shift_s, swa_s = run_layers(x_sample, cache_mem_kv, state_rwkv_wkv, state_rwkv_shift, swa_bufs, W)
    return (y_prompt, y_sample, mem_kv_prompt, wkv_p, shift_p, swa_p[0], swa_p[1], swa_p[2],
            wkv_s, shift_s, swa_s[0], swa_s[1], swa_s[2])
```

```python
import functools

import jax
import jax.numpy as jnp
from jax import lax
from jax.experimental import pallas as pl
from jax.experimental.pallas import tpu as pltpu

F32 = jnp.float32
BF16 = jnp.bfloat16

NORM_EPS = 1e-6
RW_HEAD = 64
RW_LN_EPS = RW_HEAD * 1e-5
SWA_WINDOWS = (128, 512, 2048)
SWA_DILATIONS = (1, 4, 16)
N_GROUPS = 3
SWA_HEADS = 8
SWA_HEAD_DIM = 64
SWA_WIDTH = SWA_HEADS * SWA_HEAD_DIM
SWA_BLOCK = 128
XA_HEADS = 4
XA_HEAD_DIM = 128
XA_WIDTH = XA_HEADS * XA_HEAD_DIM
WKV_CHUNK = 64
NEG_BIG = -1e30
VMEM_LIMIT = 56 * 1024 * 1024


def _cparams(sem):
    return pltpu.CompilerParams(dimension_semantics=sem, vmem_limit_bytes=VMEM_LIMIT)


def _bdot(a, b):
    return jnp.dot(a.astype(BF16), b.astype(BF16), preferred_element_type=F32)


def _bdot_nt(a, b):
    return lax.dot_general(a.astype(BF16), b.astype(BF16), (((1,), (1,)), ((), ())),
                           preferred_element_type=F32)


def _bdot_tn(a, b):
    return lax.dot_general(a.astype(BF16), b.astype(BF16), (((0,), (0,)), ((), ())),
                           preferred_element_type=F32)


def _split3(x):
    h1 = x.astype(BF16)
    r1 = x - h1.astype(F32)
    h2 = r1.astype(BF16)
    h3 = (r1 - h2.astype(F32)).astype(BF16)
    return h1, h2, h3


def _sel_dot(sel_bf16, x):
    h1, h2, h3 = _split3(x)
    d = lambda h: jnp.dot(sel_bf16, h, preferred_element_type=F32)
    return d(h1) + d(h2) + d(h3)


def _dot_sel(x, sel_bf16):
    h1, h2, h3 = _split3(x)
    d = lambda h: jnp.dot(h, sel_bf16, preferred_element_type=F32)
    return d(h1) + d(h2) + d(h3)


def _rms(x, g):
    return x * lax.rsqrt(jnp.mean(x * x, axis=-1, keepdims=True) + NORM_EPS) * g


def _sigmoid(x):
    return 1.0 / (1.0 + jnp.exp(-x))


def _pick_tile(n, target):
    t = min(n, target)
    while n % t:
        t //= 2
    return t


def _ffn_kernel(x_ref, g_ref, wg_ref, wu_ref, wo_ref, o_ref, xn_ref, acc_ref):
    j = pl.program_id(1)

    @pl.when(j == 0)
    def _():
        xn_ref[...] = _rms(x_ref[...], g_ref[...]).astype(BF16)
        acc_ref[...] = jnp.zeros_like(acc_ref)

    xn = xn_ref[...]
    gate = jnp.dot(xn, wg_ref[...], preferred_element_type=F32)
    up = jnp.dot(xn, wu_ref[...], preferred_element_type=F32)
    h = (gate * _sigmoid(gate) * up).astype(BF16)
    acc_ref[...] += jnp.dot(h, wo_ref[...], preferred_element_type=F32)

    @pl.when(j == pl.num_programs(1) - 1)
    def _():
        o_ref[...] = x_ref[...] + 0.5 * acc_ref[...]


def _ffn(x, g, w_in, w_out, *, tm_target=1024, tf=256):
    n, d = x.shape
    d_ff = w_out.shape[0]
    tm = _pick_tile(n, tm_target)
    nf = d_ff // tf
    return pl.pallas_call(
        _ffn_kernel,
        grid=(n // tm, nf),
        in_specs=[
            pl.BlockSpec((tm, d), lambda i, j: (i, 0)),
            pl.BlockSpec((1, d), lambda i, j: (0, 0)),
            pl.BlockSpec((d, tf), lambda i, j: (0, j)),
            pl.BlockSpec((d, tf), lambda i, j: (0, j + nf)),
            pl.BlockSpec((tf, d), lambda i, j: (j, 0)),
        ],
        out_specs=pl.BlockSpec((tm, d), lambda i, j: (i, 0)),
        out_shape=jax.ShapeDtypeStruct((n, d), F32),
        scratch_shapes=[pltpu.VMEM((tm, d), BF16), pltpu.VMEM((tm, d), F32)],
        compiler_params=_cparams(("parallel", "arbitrary")),
        name="ffn",
    )(x, g.reshape(1, d), w_in, w_in, w_out)


def _linear_kernel(*refs, mode, has_res):
    it = iter(refs)
    x_ref = next(it)
    p_ref = next(it) if mode in ("norm", "mul") else None
    w_ref = next(it)
    res_ref = next(it) if has_res else None
    o_ref = next(it)
    xs_ref = next(it)

    @pl.when(pl.program_id(1) == 0)
    def _():
        x = x_ref[...]
        if mode == "norm":
            x = _rms(x, p_ref[...])
        elif mode == "mul":
            x = x * p_ref[...]
        xs_ref[...] = x.astype(BF16)

    o = jnp.dot(xs_ref[...], w_ref[...], preferred_element_type=F32)
    if has_res:
        o = res_ref[...] + o
    o_ref[...] = o


def _linear(x, w, *, norm_g=None, mul=None, res=None, tm_target=512, tn_target=1024, name="linear"):
    n, kd = x.shape
    nout = w.shape[1]
    tm = _pick_tile(n, tm_target)
    tn = _pick_tile(nout, tn_target)
    mode = "norm" if norm_g is not None else ("mul" if mul is not None else "plain")
    args = [x]
    specs = [pl.BlockSpec((tm, kd), lambda i, j: (i, 0))]
    if mode == "norm":
        args.append(norm_g.reshape(1, kd))
        specs.append(pl.BlockSpec((1, kd), lambda i, j: (0, 0)))
    elif mode == "mul":
        args.append(mul)
        specs.append(pl.BlockSpec((tm, kd), lambda i, j: (i, 0)))
    args.append(w)
    specs.append(pl.BlockSpec((kd, tn), lambda i, j: (0, j)))
    if res is not None:
        args.append(res)
        specs.append(pl.BlockSpec((tm, tn), lambda i, j: (i, j)))
    return pl.pallas_call(
        functools.partial(_linear_kernel, mode=mode, has_res=res is not None),
        grid=(n // tm, nout // tn),
        in_specs=specs,
        out_specs=pl.BlockSpec((tm, tn), lambda i, j: (i, j)),
        out_shape=jax.ShapeDtypeStruct((n, nout), F32),
        scratch_shapes=[pltpu.VMEM((tm, kd), BF16)],
        compiler_params=_cparams(("parallel", "arbitrary")),
        name=name,
    )(*args)


def _rmsnorm_kernel(x_ref, g_ref, o_ref):
    o_ref[...] = _rms(x_ref[...], g_ref[...])


def _rmsnorm(x, g, *, tm_target=1024):
    n, d = x.shape
    tm = _pick_tile(n, tm_target)
    return pl.pallas_call(
        _rmsnorm_kernel,
        grid=(n // tm,),
        in_specs=[pl.BlockSpec((tm, d), lambda i: (i, 0)), pl.BlockSpec((1, d), lambda i: (0, 0))],
        out_specs=pl.BlockSpec((tm, d), lambda i: (i, 0)),
        out_shape=jax.ShapeDtypeStruct((n, d), F32),
        compiler_params=_cparams(("parallel",)),
        name="final_norm",
    )(x, g.reshape(1, d))


def _xa_prompt_kernel(x_ref, g_ref, wq_ref, kv_ref, wo_ref, o_ref):
    x = x_ref[...]
    q = jnp.dot(_rms(x, g_ref[...]).astype(BF16), wq_ref[...], preferred_element_type=F32)
    kv = kv_ref[...].astype(BF16)
    outs = []
    for h in range(XA_HEADS):
        lo = h * XA_HEAD_DIM
        s = _bdot_nt(q[:, lo:lo + XA_HEAD_DIM], kv[:, lo:lo + XA_HEAD_DIM]) * (XA_HEAD_DIM ** -0.5)
        m = jnp.max(s, axis=-1, keepdims=True)
        p = jnp.exp(s - m)
        l = jnp.sum(p, axis=-1, keepdims=True)
        o = jnp.dot(p.astype(BF16), kv[:, XA_WIDTH + lo:XA_WIDTH + lo + XA_HEAD_DIM],
                    preferred_element_type=F32)
        outs.append(o / l)
    o = jnp.concatenate(outs, axis=-1)
    o_ref[...] = x + jnp.dot(o.astype(BF16), wo_ref[...], preferred_element_type=F32)


def _xa_prompt(x, g, wq, kv, wo, *, seq, tm_target=512):
    n, d = x.shape
    n_mem = kv.shape[1]
    tm = _pick_tile(seq, tm_target)
    per_b = seq // tm
    return pl.pallas_call(
        _xa_prompt_kernel,
        grid=(n // tm,),
        in_specs=[
            pl.BlockSpec((tm, d), lambda i: (i, 0)),
            pl.BlockSpec((1, d), lambda i: (0, 0)),
            pl.BlockSpec((d, XA_WIDTH), lambda i: (0, 0)),
            pl.BlockSpec((None, n_mem, 2 * XA_WIDTH), lambda i: (i // per_b, 0, 0)),
            pl.BlockSpec((XA_WIDTH, d), lambda i: (0, 0)),
        ],
        out_specs=pl.BlockSpec((tm, d), lambda i: (i, 0)),
        out_shape=jax.ShapeDtypeStruct((n, d), F32),
        compiler_params=_cparams(("parallel",)),
        name="xa_prompt",
    )(x, g.reshape(1, d), wq, kv, wo)


def _xa_sample_kernel(q_ref, kv_ref, o_ref):
    q = q_ref[...]
    outs = []
    for h in range(XA_HEADS):
        lo = h * XA_HEAD_DIM
        k = kv_ref[:, :, lo:lo + XA_HEAD_DIM]
        v = kv_ref[:, :, XA_WIDTH + lo:XA_WIDTH + lo + XA_HEAD_DIM]
        s = jnp.sum(k * q[:, :, lo:lo + XA_HEAD_DIM], axis=-1, keepdims=True) * (XA_HEAD_DIM ** -0.5)
        m = jnp.max(s, axis=1, keepdims=True)
        p = jnp.exp(s - m)
        l = jnp.sum(p, axis=1, keepdims=True)
        outs.append(jnp.sum(p * v, axis=1, keepdims=True) / l)
    o_ref[...] = jnp.concatenate(outs, axis=-1)


def _xa_sample(q, kv, *, bb=8):
    b = q.shape[0]
    n_mem = kv.shape[1]
    bb = _pick_tile(b, bb)
    out = pl.pallas_call(
        _xa_sample_kernel,
        grid=(b // bb,),
        in_specs=[
            pl.BlockSpec((bb, 1, XA_WIDTH), lambda i: (i, 0, 0)),
            pl.BlockSpec((bb, n_mem, 2 * XA_WIDTH), lambda i: (i, 0, 0)),
        ],
        out_specs=pl.BlockSpec((bb, 1, XA_WIDTH), lambda i: (i, 0, 0)),
        out_shape=jax.ShapeDtypeStruct((b, 1, XA_WIDTH), F32),
        compiler_params=_cparams(("parallel",)),
        name="xa_sample",
    )(q.reshape(b, 1, XA_WIDTH), kv)
    return out.reshape(b, XA_WIDTH)


def _band_kernel(q_ref, kp_ref, kc_ref, vp_ref, vc_ref, o_ref, l_ref, *, dil, group):
    n = pl.program_id(2)
    q = (q_ref[...] * (SWA_HEAD_DIM ** -0.5)).astype(BF16)
    k = jnp.concatenate([kp_ref[...], kc_ref[...]], axis=0).astype(BF16)
    v = jnp.concatenate([vp_ref[...], vc_ref[...]], axis=0).astype(BF16)
    qi = lax.broadcasted_iota(jnp.int32, (SWA_BLOCK, 2 * SWA_BLOCK), 0)
    ki = lax.broadcasted_iota(jnp.int32, (SWA_BLOCK, 2 * SWA_BLOCK), 1)
    delta = qi + SWA_BLOCK - ki
    units = SWA_WINDOWS[group] // dil
    k_min = jnp.where(n > 0, 0, SWA_BLOCK)
    valid = (delta >= 0) & (delta <= units) & (ki >= k_min)
    dist = (dil * delta).astype(F32)
    for h in range(SWA_HEADS):
        lo = h * SWA_HEAD_DIM
        slope = 2.0 ** (-8.0 * (group * SWA_HEADS + h + 1) / (N_GROUPS * SWA_HEADS))
        s = _bdot_nt(q[:, lo:lo + SWA_HEAD_DIM], k[:, lo:lo + SWA_HEAD_DIM])
        s = jnp.where(valid, s - slope * dist, NEG_BIG)
        m = jnp.max(s, axis=-1, keepdims=True)
        p = jnp.exp(s - m)
        l = jnp.sum(p, axis=-1, keepdims=True)
        o = jnp.dot(p.astype(BF16), v[:, lo:lo + SWA_HEAD_DIM], preferred_element_type=F32) / l
        o_ref[:, lo:lo + SWA_HEAD_DIM] = o
        l_ref[:, lo:lo + SWA_HEAD_DIM] = jnp.broadcast_to(m + jnp.log(l), (SWA_BLOCK, SWA_HEAD_DIM))


def _band_attention(qkv, group, *, bsz, seq):
    dil = SWA_DILATIONS[group]
    width = qkv.shape[1]
    l_sub = seq // dil
    nb = l_sub // SWA_BLOCK
    qkv_v = qkv.reshape(bsz, l_sub, dil * width)
    cpb = width // SWA_WIDTH
    base = group * 3

    def spec(off, prev):
        if prev:
            return pl.BlockSpec((None, SWA_BLOCK, SWA_WIDTH),
                                lambda b, r, n: (b, jnp.maximum(n - 1, 0), r * cpb + base + off))
        return pl.BlockSpec((None, SWA_BLOCK, SWA_WIDTH), lambda b, r, n: (b, n, r * cpb + base + off))

    out_spec = pl.BlockSpec((None, SWA_BLOCK, SWA_WIDTH), lambda b, r, n: (b, n, r))
    o, lse = pl.pallas_call(
        functools.partial(_band_kernel, dil=dil, group=group),
        grid=(bsz, dil, nb),
        in_specs=[spec(0, False), spec(1, True), spec(1, False), spec(2, True), spec(2, False)],
        out_specs=[out_spec, out_spec],
        out_shape=[jax.ShapeDtypeStruct((bsz, l_sub, dil * SWA_WIDTH), F32)] * 2,
        compiler_params=_cparams(("parallel", "parallel", "arbitrary")),
        name="swa_band_g%d" % group,
    )(qkv_v, qkv_v, qkv_v, qkv_v, qkv_v)
    return o.reshape(bsz * seq, SWA_WIDTH), lse.reshape(bsz * seq, SWA_WIDTH)


def _merge_kernel(o0, o1, o2, l0, l1, l2, x_ref, w_ref, out_ref):
    a0, a1, a2 = l0[...], l1[...], l2[...]
    m = jnp.maximum(jnp.maximum(a0, a1), a2)
    e0, e1, e2 = jnp.exp(a0 - m), jnp.exp(a1 - m), jnp.exp(a2 - m)
    y = (e0 * o0[...] + e1 * o1[...] + e2 * o2[...]) / (e0 + e1 + e2)
    out_ref[...] = x_ref[...] + jnp.dot(y.astype(BF16), w_ref[...], preferred_element_type=F32)


def _swa_merge(outs, lses, x, wo, *, tm_target=512):
    n, d = x.shape
    tm = _pick_tile(n, tm_target)
    hs = pl.BlockSpec((tm, SWA_WIDTH), lambda i: (i, 0))
    return pl.pallas_call(
        _merge_kernel,
        grid=(n // tm,),
        in_specs=[hs] * 6 + [pl.BlockSpec((tm, d), lambda i: (i, 0)),
                             pl.BlockSpec((SWA_WIDTH, d), lambda i: (0, 0))],
        out_specs=pl.BlockSpec((tm, d), lambda i: (i, 0)),
        out_shape=jax.ShapeDtypeStruct((n, d), F32),
        compiler_params=_cparams(("parallel",)),
        name="swa_merge",
    )(*outs, *lses, x, wo)


def _gather_kernel(qkv_ref, c0_ref, c1_ref, c2_ref, y_ref):
    hi = lax.broadcasted_iota(jnp.int32, (1, 1, SWA_HEADS, 1), 2).astype(F32)
    mi = lax.broadcasted_iota(jnp.int32, (1, SWA_BLOCK, 1, 1), 1).astype(F32)
    outs, lses = [], []
    for g, c_ref in enumerate((c0_ref, c1_ref, c2_ref)):
        dil = SWA_DILATIONS[g]
        q = qkv_ref[:, 3 * g] * (SWA_HEAD_DIM ** -0.5)
        k_new = qkv_ref[:, 3 * g + 1]
        v_new = qkv_ref[:, 3 * g + 2]
        slope = jnp.exp2(-8.0 * (g * SWA_HEADS + hi + 1.0) / (N_GROUPS * SWA_HEADS))
        kc = c_ref[:, :, 0]
        vc = c_ref[:, :, 1]
        s = jnp.sum(kc * q[:, None], axis=-1, keepdims=True)
        s = s - slope * (dil * (SWA_BLOCK - mi))
        s0 = jnp.sum(k_new * q, axis=-1, keepdims=True)
        m = jnp.maximum(jnp.max(s, axis=1), s0)
        p = jnp.exp(s - m[:, None])
        p0 = jnp.exp(s0 - m)
        l = jnp.sum(p, axis=1) + p0
        o = (jnp.sum(p * vc, axis=1) + p0 * v_new) / l
        outs.append(o)
        lses.append(m + jnp.log(l))
    mm = jnp.maximum(jnp.maximum(lses[0], lses[1]), lses[2])
    es = [jnp.exp(z - mm) for z in lses]
    y_ref[...] = (es[0] * outs[0] + es[1] * outs[1] + es[2] * outs[2]) / (es[0] + es[1] + es[2])


def _gather_attention(qkv, caches, layer_idx, *, bb=4):
    b = qkv.shape[0]
    bb = _pick_tile(b, bb)
    qkv_v = qkv.reshape(b, 3 * N_GROUPS, SWA_HEADS, SWA_HEAD_DIM)
    views, specs = [], [pl.BlockSpec((bb, 3 * N_GROUPS, SWA_HEADS, SWA_HEAD_DIM), lambda i: (i, 0, 0, 0))]
    for g, c in enumerate(caches):
        dil = SWA_DILATIONS[g]
        n_att, _, w = c.shape[:3]
        views.append(c.reshape(n_att, b, w // dil, dil, 2, SWA_HEADS, SWA_HEAD_DIM))
        specs.append(pl.BlockSpec((None, bb, w // dil, None, 2, SWA_HEADS, SWA_HEAD_DIM),
                                  lambda i: (layer_idx, i, 0, 0, 0, 0, 0)))
    y = pl.pallas_call(
        _gather_kernel,
        grid=(b // bb,),
        in_specs=specs,
        out_specs=pl.BlockSpec((bb, SWA_HEADS, SWA_HEAD_DIM), lambda i: (i, 0, 0)),
        out_shape=jax.ShapeDtypeStruct((b, SWA_HEADS, SWA_HEAD_DIM), F32),
        compiler_params=_cparams(("parallel",)),
        name="swa_gather",
    )(qkv_v, *views)
    return y.reshape(b, SWA_WIDTH)


def _shift_kernel(c_ref, nxt_ref, new_ref, o_ref):
    wb = c_ref.shape[1]
    j = pl.program_id(2)
    o_ref[:, 0:wb - 1] = c_ref[:, 1:wb]
    is_last = j == pl.num_programs(2) - 1
    o_ref[:, wb - 1] = jnp.where(is_last, new_ref[...], nxt_ref[:, 0])


def _shift_cache(cache, new_rows, *, block_bytes=4 * 1024 * 1024):
    n_att, b, w = cache.shape[:3]
    row_bytes = 2 * SWA_HEADS * 128 * 4
    wb = _pick_tile(w, max(1, block_bytes // row_bytes))
    bb = _pick_tile(b, max(1, block_bytes // (wb * row_bytes)))
    tail = (2, SWA_HEADS, SWA_HEAD_DIM)
    return pl.pallas_call(
        _shift_kernel,
        grid=(n_att, b // bb, w // wb),
        in_specs=[pl.BlockSpec((None, bb, wb) + tail, lambda l, i, j: (l, i, j, 0, 0, 0)),
                  pl.BlockSpec((None, bb, 1) + tail,
                               lambda l, i, j: (l, i, jnp.minimum((j + 1) * wb, w - 1), 0, 0, 0)),
                  pl.BlockSpec((None, bb) + tail, lambda l, i, j: (l, i, 0, 0, 0))],
        out_specs=pl.BlockSpec((None, bb, wb) + tail, lambda l, i, j: (l, i, j, 0, 0, 0)),
        out_shape=jax.ShapeDtypeStruct(cache.shape, cache.dtype),
        compiler_params=_cparams(("parallel", "parallel", "arbitrary")),
        name="swa_cache_shift",
    )(cache, cache, new_rows)


def _rwkv_proj_kernel(*refs, seq_mode, has_vres, tiles_per_seq):
    it = iter(refs)
    x_ref = next(it)
    prev_ref = next(it)
    vfirst_ref = next(it) if has_vres else None
    g_ref, mix_ref, wr_ref, wk_ref, wv_ref = next(it), next(it), next(it), next(it), next(it)
    w0_ref, w1_ref, w2_ref = next(it), next(it), next(it)
    a0_ref, a1_ref, a2_ref = next(it), next(it), next(it)
    if has_vres:
        v0_ref, v1_ref, v2_ref = next(it), next(it), next(it)
    g1_ref, g2_ref, kk_ref, ka_ref = next(it), next(it), next(it), next(it)
    hsel_ref, hselt_ref = next(it), next(it)
    r_out, ld_out, k_out, v_out, a_out, b_out, g_out, xn_last_out = (next(it) for _ in range(8))

    xn = _rms(x_ref[...], g_ref[...])
    tm = xn.shape[0]
    if seq_mode:
        i = pl.program_id(0)
        pr = _rms(prev_ref[...], g_ref[...])[7:8]
        pr = jnp.where(i % tiles_per_seq == 0, 0.0, pr)
        row = lax.broadcasted_iota(jnp.int32, xn.shape, 0)
        xp = jnp.where(row == 0, pr, pltpu.roll(xn, 1, axis=0))
    else:
        xp = prev_ref[...]
    xx = xp - xn
    mix = mix_ref[...]
    xr, xw, xk, xv, xa, xg = (xn + xx * mix[j:j + 1] for j in range(6))
    r = _bdot(xr, wr_ref[...])
    k = _bdot(xk, wk_ref[...])
    v = _bdot(xv, wv_ref[...])
    z = w0_ref[...] + _bdot(jnp.tanh(_bdot(xw, w1_ref[...])), w2_ref[...])
    w_log = -(jnp.maximum(-z, 0.0) + jnp.log(1.0 + jnp.exp(-jnp.abs(z)))) - 0.5
    ld = -jnp.exp(w_log)
    a = _sigmoid(a0_ref[...] + _bdot(_bdot(xa, a1_ref[...]), a2_ref[...]))
    g = _bdot(_sigmoid(_bdot(xg, g1_ref[...])), g2_ref[...])
    if has_vres:
        v = v + (vfirst_ref[...] - v) * _sigmoid(v0_ref[...] + _bdot(_bdot(xv, v1_ref[...]), v2_ref[...]))
    kk = k * kk_ref[...]
    ssq = jnp.dot((kk * kk).astype(BF16), hsel_ref[...], preferred_element_type=F32)
    rs = lax.rsqrt(jnp.maximum(ssq, 1e-24))
    kk = kk * _dot_sel(rs, hselt_ref[...])
    k = k * (1.0 + (a - 1.0) * ka_ref[...])
    r_out[...] = r
    ld_out[...] = ld
    k_out[...] = k
    v_out[...] = v
    a_out[...] = -kk
    b_out[...] = kk * a
    g_out[...] = g
    xn_last_out[...] = xn[tm - 8:tm] if seq_mode else xn


def _head_selectors(d):
    heads = d // RW_HEAD
    ch = jnp.arange(d) // RW_HEAD
    hsel = (ch[:, None] == jnp.arange(128)[None, :]).astype(BF16)
    return hsel, hsel.T


def _rwkv_proj(x, prev, v_first, p, *, seq_mode, seq, tm_target=256):
    n, d = x.shape
    tm = _pick_tile(seq if seq_mode else n, tm_target)
    has_vres = v_first is not None
    row = lambda a: a.reshape(1, -1)
    full = lambda a: pl.BlockSpec(a.shape, lambda i: (0,) * a.ndim)
    tile = pl.BlockSpec((tm, d), lambda i: (i, 0))
    args, specs = [x], [tile]
    if seq_mode:
        args.append(x)
        specs.append(pl.BlockSpec((8, d), lambda i: (jnp.maximum(i * (tm // 8) - 1, 0), 0)))
    else:
        args.append(prev)
        specs.append(tile)
    if has_vres:
        args.append(v_first)
        specs.append(tile)
    hsel, hselt = _head_selectors(d)
    consts = [row(p["norm_g"]), p["mix"], p["w_r"], p["w_k"], p["w_v"],
              row(p["w0"]), p["w1"], p["w2"], row(p["a0"]), p["a1"], p["a2"]]
    if has_vres:
        consts += [row(p["v0"]), p["v1"], p["v2"]]
    consts += [p["g1"], p["g2"], row(p["k_k"]), row(p["k_a"]), hsel, hselt]
    args += consts
    specs += [full(c) for c in consts]
    outs = pl.pallas_call(
        functools.partial(_rwkv_proj_kernel, seq_mode=seq_mode, has_vres=has_vres,
                          tiles_per_seq=(seq // tm) if seq_mode else 1),
        grid=(n // tm,),
        in_specs=specs,
        out_specs=[tile] * 7 + [pl.BlockSpec((8, d), lambda i: (i, 0)) if seq_mode else tile],
        out_shape=[jax.ShapeDtypeStruct((n, d), F32)] * 7
        + [jax.ShapeDtypeStruct((n // tm * 8 if seq_mode else n, d), F32)],
        compiler_params=_cparams(("parallel",)),
        name="rwkv_proj",
    )(*args)
    return outs


def _wkv_chunk_kernel(r_ref, ld_ref, k_ref, v_ref, a_ref, b_ref, lnw_ref, lnb_ref, rk_ref,
                      y_ref, s_ref, h_ref, *, n_chunks):
    C = WKV_CHUNK
    t = pl.program_id(2)

    @pl.when(t == 0)
    def _():
        h_ref[...] = jnp.zeros_like(h_ref)

    ri = lax.broadcasted_iota(jnp.int32, (C, C), 0)
    ci = lax.broadcasted_iota(jnp.int32, (C, C), 1)
    tril = ri >= ci
    stril = ri > ci
    tri_ones = tril.astype(BF16)
    lnw, lnb, rk = lnw_ref[...], lnb_ref[...], rk_ref[...]

    def chunk(c, carry):
        sl = pl.ds(pl.multiple_of(c * C, C), C)
        rc, ldc, kc, vc, ac, bc = (z[sl, :] for z in (r_ref, ld_ref, k_ref, v_ref, a_ref, b_ref))
        cum = _sel_dot(tri_ones, ldc)
        e_pos = jnp.exp(cum)
        e_neg = jnp.exp(-cum)
        at = ac * jnp.exp(cum - ldc)
        rt = rc * e_pos
        bt = bc * e_neg
        kt = kc * e_neg
        last = cum[C - 1:C]
        e_last = jnp.exp(last - cum)
        bh = bc * e_last
        kh = kc * e_last
        dcol = jnp.transpose(jnp.broadcast_to(jnp.exp(last), (128, 128)))
        for h in range(2):
            lo = h * RW_HEAD
            hs = slice(lo, lo + RW_HEAD)
            h0 = h_ref[h]
            at_h, rt_h, bt_h, kt_h, v_h = at[:, hs], rt[:, hs], bt[:, hs], kt[:, hs], vc[:, hs]
            ab = jnp.where(stril, _bdot_nt(at_h, bt_h), 0.0)
            ak = jnp.where(stril, _bdot_nt(at_h, kt_h), 0.0)
            rb = jnp.where(tril, _bdot_nt(rt_h, bt_h), 0.0)
            rkm = jnp.where(tril, _bdot_nt(rt_h, kt_h), 0.0)
            u = _bdot(at_h, h0) + _bdot(ak, v_h)
            lp = ab
            n = 1
            while n < C:
                u = u + _bdot(lp, u)
                n *= 2
                if n < C:
                    lp = _bdot(lp, lp)
            y = _bdot(rt_h, h0) + _bdot(rb, u) + _bdot(rkm, v_h)
            h_ref[h] = dcol[lo:lo + RW_HEAD, 0:RW_HEAD] * h0 + _bdot_tn(bh[:, hs], u) + _bdot_tn(kh[:, hs], v_h)
            mu = jnp.mean(y, axis=-1, keepdims=True)
            yc = y - mu
            var = jnp.mean(yc * yc, axis=-1, keepdims=True)
            yn = yc * lax.rsqrt(var + RW_LN_EPS) * lnw[:, hs] + lnb[:, hs]
            bonus = jnp.sum(rc[:, hs] * kc[:, hs] * rk[:, hs], axis=-1, keepdims=True) * v_h
            y_ref[sl, hs] = yn + bonus
        return carry

    lax.fori_loop(0, n_chunks, chunk, 0)

    @pl.when(t == pl.num_programs(2) - 1)
    def _():
        s_ref[...] = h_ref[...]


def _wkv_chunked(r, ld, k, v, a, b, ln_w, ln_b, r_k, *, bsz, seq, tb_target=512):
    n, d = r.shape
    heads = d // RW_HEAD
    pairs = heads // 2
    tb = _pick_tile(seq, tb_target)
    per_seq = seq // tb
    tile = pl.BlockSpec((tb, 128), lambda bi, hp, t: (bi * per_seq + t, hp))
    vec = pl.BlockSpec((1, 128), lambda bi, hp, t: (0, hp))
    y, st = pl.pallas_call(
        functools.partial(_wkv_chunk_kernel, n_chunks=tb // WKV_CHUNK),
        grid=(bsz, pairs, per_seq),
        in_specs=[tile] * 6 + [vec] * 3,
        out_specs=[tile, pl.BlockSpec((None, 2, RW_HEAD, RW_HEAD), lambda bi, hp, t: (bi, hp, 0, 0))],
        out_shape=[jax.ShapeDtypeStruct((n, d), F32),
                   jax.ShapeDtypeStruct((bsz, heads, RW_HEAD, RW_HEAD), F32)],
        scratch_shapes=[pltpu.VMEM((2, RW_HEAD, RW_HEAD), F32)],
        compiler_params=_cparams(("parallel", "parallel", "arbitrary")),
        name="wkv_chunked",
    )(r, ld, k, v, a, b, ln_w.reshape(1, d), ln_b.reshape(1, d), r_k.reshape(1, d))
    return y, jnp.swapaxes(st, -1, -2)


def _wkv_step_kernel(s_ref, r_ref, ld_ref, k_ref, a_ref, b_ref, rk_ref, v_ref, lnw_ref, lnb_ref,
                     s_out, y_out):
    s = s_ref[...]
    r, k, a, b = r_ref[...], k_ref[...], a_ref[...], b_ref[...]
    w = jnp.exp(ld_ref[...])
    v = v_ref[...]
    sa = jnp.sum(s * a, axis=-1, keepdims=True)
    s_new = s * w + sa * b + v * k
    y = jnp.sum(s_new * r, axis=-1, keepdims=True)
    mu = jnp.mean(y, axis=-2, keepdims=True)
    yc = y - mu
    var = jnp.mean(yc * yc, axis=-2, keepdims=True)
    yn = yc * lax.rsqrt(var + RW_LN_EPS) * lnw_ref[...] + lnb_ref[...]
    bonus = jnp.sum(r * k * rk_ref[...], axis=-1, keepdims=True) * v
    s_out[...] = s_new
    y_out[...] = yn + bonus


def _wkv_step(state, r, ld, k, v, a, b, ln_w, ln_b, r_k, *, bb=8):
    bsz, heads = state.shape[:2]
    bb = _pick_tile(bsz, bb)
    rows = lambda z: z.reshape(bsz, heads, 1, RW_HEAD)
    cols = lambda z: z.reshape(bsz, heads, RW_HEAD, 1)
    st_spec = pl.BlockSpec((bb, heads, RW_HEAD, RW_HEAD), lambda i: (i, 0, 0, 0))
    row_spec = pl.BlockSpec((bb, heads, 1, RW_HEAD), lambda i: (i, 0, 0, 0))
    col_spec = pl.BlockSpec((bb, heads, RW_HEAD, 1), lambda i: (i, 0, 0, 0))
    prow_spec = pl.BlockSpec((1, heads, 1, RW_HEAD), lambda i: (0, 0, 0, 0))
    pcol_spec = pl.BlockSpec((1, heads, RW_HEAD, 1), lambda i: (0, 0, 0, 0))
    s_new, y = pl.pallas_call(
        _wkv_step_kernel,
        grid=(bsz // bb,),
        in_specs=[st_spec] + [row_spec] * 5 + [prow_spec, col_spec, pcol_spec, pcol_spec],
        out_specs=[st_spec, col_spec],
        out_shape=[jax.ShapeDtypeStruct(state.shape, F32),
                   jax.ShapeDtypeStruct((bsz, heads, RW_HEAD, 1), F32)],
        compiler_params=_cparams(("parallel",)),
        name="wkv_step",
    )(state, rows(r), rows(ld), rows(k), rows(a), rows(b), r_k.reshape(1, heads, 1, RW_HEAD),
      cols(v), ln_w.reshape(1, heads, RW_HEAD, 1), ln_b.reshape(1, heads, RW_HEAD, 1))
    return y.reshape(bsz, heads * RW_HEAD), s_new


def _run_layers(x, mem_kv, wkv0, shift0, swa_bufs, W, *, bsz, seq):
    is_prompt = swa_bufs is None
    depth = W["norm_g"].shape[0]
    new_wkv, new_shift = [], []
    new_kv = [[] for _ in range(N_GROUPS)]
    v_first = None
    for l in range(depth):
        ng = W["norm_g"][l]
        x = _ffn(x, ng[0], W["ffn_w_in"][l, 0], W["ffn_w_out"][l, 0])
        i = l // 2
        if l % 2 == 0:
            p = dict(norm_g=ng[1], mix=W["rw_mix"][i], w_r=W["rw_w_rkv"][i, 0], w_k=W["rw_w_rkv"][i, 1],
                     w_v=W["rw_w_rkv"][i, 2], w0=W["rw_w0"][i], w1=W["rw_w1"][i], w2=W["rw_w2"][i],
                     a0=W["rw_a0"][i], a1=W["rw_a1"][i], a2=W["rw_a2"][i], g1=W["rw_g1"][i], g2=W["rw_g2"][i],
                     k_k=W["rw_k_k"][i], k_a=W["rw_k_a"][i])
            if i > 0:
                p.update(v0=W["rw_v0"][i - 1], v1=W["rw_v1"][i - 1], v2=W["rw_v2"][i - 1])
            prev = None if is_prompt else shift0[i]
            r, ld, k, v, a, b, g, xn_last = _rwkv_proj(x, prev, v_first if i > 0 else None, p,
                                                      seq_mode=is_prompt, seq=seq)
            if i == 0:
                v_first = v
            if is_prompt:
                y, s_new = _wkv_chunked(r, ld, k, v, a, b, W["rw_ln_w"][i], W["rw_ln_b"][i],
                                        W["rw_r_k"][i].reshape(-1), bsz=bsz, seq=seq)
                tiles = xn_last.shape[0] // 8 // bsz
                shift_new = xn_last.reshape(bsz, tiles, 8, -1)[:, -1, -1]
            else:
                y, s_new = _wkv_step(wkv0[i], r, ld, k, v, a, b, W["rw_ln_w"][i], W["rw_ln_b"][i],
                                     W["rw_r_k"][i].reshape(-1))
                shift_new = xn_last
            new_wkv.append(s_new)
            new_shift.append(shift_new)
            x = _linear(y, W["rw_w_o"][i], mul=g, res=x, name="rwkv_out")
        else:
            qkv = _linear(x, W["swa_w_qkv"][i], norm_g=ng[1], tn_target=512, name="swa_qkv")
            qkv5 = qkv.reshape(bsz, seq, N_GROUPS, 3, SWA_WIDTH)
            if is_prompt:
                outs, lses = [], []
                for gi in range(N_GROUPS):
                    o, lse = _band_attention(qkv, gi, bsz=bsz, seq=seq)
                    outs.append(o)
                    lses.append(lse)
                    keep = min(SWA_WINDOWS[gi], seq)
                    new_kv[gi].append(qkv5[:, seq - keep:, gi, 1:3].reshape(
                        bsz, keep, 2, SWA_HEADS, SWA_HEAD_DIM))
                x = _swa_merge(outs, lses, x, W["swa_w_o"][i])
            else:
                y = _gather_attention(qkv, swa_bufs, i)
                for gi in range(N_GROUPS):
                    new_kv[gi].append(qkv5[:, 0, gi, 1:3].reshape(bsz, 2, SWA_HEADS, SWA_HEAD_DIM))
                x = _linear(y, W["swa_w_o"][i], res=x, name="swa_out")
        if is_prompt:
            x = _xa_prompt(x, ng[2], W["xa_w_q"][l], mem_kv[l], W["xa_w_o"][l], seq=seq)
        else:
            q = _linear(x, W["xa_w_q"][l], norm_g=ng[2], name="xa_q")
            o = _xa_sample(q, mem_kv[l])
            x = _linear(o, W["xa_w_o"][l], res=x, name="xa_out")
        x = _ffn(x, ng[3], W["ffn_w_in"][l, 1], W["ffn_w_out"][l, 1])
    y = _rmsnorm(x, W["final_norm_g"])
    return y, new_wkv, new_shift, new_kv


def kernel(x_prompt, x_sample, cache_mem_kv, state_rwkv_wkv, state_rwkv_shift, cache_swa_kv_g0, cache_swa_kv_g1, cache_swa_kv_g2, mem_prompt, norm_g, mem_norm_g, final_norm_g, ffn_w_in, ffn_w_out, rw_mix, rw_w_rkv, rw_w_o, rw_w0, rw_w1, rw_w2, rw_a0, rw_a1, rw_a2, rw_v0, rw_v1, rw_v2, rw_g1, rw_g2, rw_k_k, rw_k_a, rw_r_k, rw_ln_w, rw_ln_b, swa_w_qkv, swa_w_o, xa_w_q, xa_w_kv, xa_w_o):
    bf = lambda w: w.astype(BF16)
    W = dict(norm_g=norm_g, final_norm_g=final_norm_g, ffn_w_in=bf(ffn_w_in), ffn_w_out=bf(ffn_w_out),
             rw_mix=rw_mix, rw_w_rkv=bf(rw_w_rkv), rw_w_o=bf(rw_w_o), rw_w0=rw_w0, rw_w1=bf(rw_w1),
             rw_w2=bf(rw_w2), rw_a0=rw_a0, rw_a1=bf(rw_a1), rw_a2=bf(rw_a2), rw_v0=rw_v0, rw_v1=bf(rw_v1),
             rw_v2=bf(rw_v2), rw_g1=bf(rw_g1), rw_g2=bf(rw_g2), rw_k_k=rw_k_k, rw_k_a=rw_k_a, rw_r_k=rw_r_k,
             rw_ln_w=rw_ln_w, rw_ln_b=rw_ln_b, swa_w_qkv=bf(swa_w_qkv), swa_w_o=bf(swa_w_o),
             xa_w_q=bf(xa_w_q), xa_w_o=bf(xa_w_o))
    depth = norm_g.shape[0]
    bsz, seq, d = x_prompt.shape
    n_mem = mem_prompt.shape[1]

    mem2 = mem_prompt.reshape(bsz * n_mem, d)
    xa_w_kv_b = bf(xa_w_kv)
    mem_kv_p = jnp.stack([_linear(mem2, xa_w_kv_b[l], norm_g=mem_norm_g[l], name="mem_kv")
                          for l in range(depth)])
    y_p, wkv_p, shift_p, kv_p = _run_layers(
        x_prompt.reshape(bsz * seq, d), mem_kv_p.reshape(depth, bsz, n_mem, 2 * XA_WIDTH),
        None, None, None, W, bsz=bsz, seq=seq)

    dbsz, dseq, _ = x_sample.shape
    assert dseq == 1
    caches = (cache_swa_kv_g0, cache_swa_kv_g1, cache_swa_kv_g2)
    y_s, wkv_s, shift_s, kv_s = _run_layers(
        x_sample.reshape(dbsz, d), cache_mem_kv.reshape(depth, dbsz, n_mem, 2 * XA_WIDTH),
        state_rwkv_wkv, state_rwkv_shift, caches, W, bsz=dbsz, seq=1)
    swa_s = tuple(_shift_cache(caches[g], jnp.stack(kv_s[g])) for g in range(N_GROUPS))

    return (y_p.reshape(bsz, seq, d), y_s.reshape(dbsz, 1, d),
            mem_kv_p.reshape(depth, bsz, n_mem, 2, XA_HEADS, XA_HEAD_DIM),
            jnp.stack(wkv_p), jnp.stack(shift_p),
            jnp.stack(kv_p[0]), jnp.stack(kv_p[1]), jnp.stack(kv_p[2]),
            jnp.stack(wkv_s), jnp.stack(shift_s), swa_s[0], swa_s[1], swa_s[2])
```

```python
import functools

import jax
import jax.numpy as jnp
from jax import lax
from jax.experimental import pallas as pl
from jax.experimental.pallas import tpu as pltpu

F32 = jnp.float32
BF16 = jnp.bfloat16

NORM_EPS = 1e-6
RW_HEAD = 64
RW_LN_EPS = RW_HEAD * 1e-5
SWA_WINDOWS = (128, 512, 2048)
SWA_DILATIONS = (1, 4, 16)
N_GROUPS = 3
SWA_HEADS = 8
SWA_HEAD_DIM = 64
SWA_WIDTH = SWA_HEADS * SWA_HEAD_DIM
SWA_BLOCK = 128
XA_HEADS = 4
XA_HEAD_DIM = 128
XA_WIDTH = XA_HEADS * XA_HEAD_DIM
WKV_CHUNK = 64
NEG_BIG = -1e30
VMEM_LIMIT = 56 * 1024 * 1024


def _cparams(sem):
    return pltpu.CompilerParams(dimension_semantics=sem, vmem_limit_bytes=VMEM_LIMIT)


def _bdot(a, b):
    return jnp.dot(a.astype(BF16), b.astype(BF16), preferred_element_type=F32)


def _bdot_nt(a, b):
    return lax.dot_general(a.astype(BF16), b.astype(BF16), (((1,), (1,)), ((), ())),
                           preferred_element_type=F32)


def _bdot_tn(a, b):
    return lax.dot_general(a.astype(BF16), b.astype(BF16), (((0,), (0,)), ((), ())),
                           preferred_element_type=F32)


def _split3(x):
    h1 = x.astype(BF16)
    r1 = x - h1.astype(F32)
    h2 = r1.astype(BF16)
    h3 = (r1 - h2.astype(F32)).astype(BF16)
    return h1, h2, h3


def _sel_dot(sel_bf16, x):
    h1, h2, h3 = _split3(x)
    d = lambda h: jnp.dot(sel_bf16, h, preferred_element_type=F32)
    return d(h1) + d(h2) + d(h3)


def _dot_sel(x, sel_bf16):
    h1, h2, h3 = _split3(x)
    d = lambda h: jnp.dot(h, sel_bf16, preferred_element_type=F32)
    return d(h1) + d(h2) + d(h3)


def _rms(x, g):
    return x * lax.rsqrt(jnp.mean(x * x, axis=-1, keepdims=True) + NORM_EPS) * g


def _sigmoid(x):
    return 1.0 / (1.0 + jnp.exp(-x))


def _pick_tile(n, target):
    t = min(n, target)
    while n % t:
        t //= 2
    return t


def _ffn_kernel(x_ref, g_ref, wg_ref, wu_ref, wo_ref, o_ref, xn_ref, acc_ref):
    j = pl.program_id(1)

    @pl.when(j == 0)
    def _():
        xn_ref[...] = _rms(x_ref[...], g_ref[...]).astype(BF16)
        acc_ref[...] = jnp.zeros_like(acc_ref)

    xn = xn_ref[...]
    gate = jnp.dot(xn, wg_ref[...], preferred_element_type=F32)
    up = jnp.dot(xn, wu_ref[...], preferred_element_type=F32)
    h = (gate * _sigmoid(gate) * up).astype(BF16)
    acc_ref[...] += jnp.dot(h, wo_ref[...], preferred_element_type=F32)

    @pl.when(j == pl.num_programs(1) - 1)
    def _():
        o_ref[...] = x_ref[...] + 0.5 * acc_ref[...]


def _ffn(x, g, w_in, w_out, *, tm_target=1024, tf=256):
    n, d = x.shape
    d_ff = w_out.shape[0]
    tm = _pick_tile(n, tm_target)
    nf = d_ff // tf
    return pl.pallas_call(
        _ffn_kernel,
        grid=(n // tm, nf),
        in_specs=[
            pl.BlockSpec((tm, d), lambda i, j: (i, 0)),
            pl.BlockSpec((1, d), lambda i, j: (0, 0)),
            pl.BlockSpec((d, tf), lambda i, j: (0, j)),
            pl.BlockSpec((d, tf), lambda i, j: (0, j + nf)),
            pl.BlockSpec((tf, d), lambda i, j: (j, 0)),
        ],
        out_specs=pl.BlockSpec((tm, d), lambda i, j: (i, 0)),
        out_shape=jax.ShapeDtypeStruct((n, d), F32),
        scratch_shapes=[pltpu.VMEM((tm, d), BF16), pltpu.VMEM((tm, d), F32)],
        compiler_params=_cparams(("parallel", "arbitrary")),
        name="ffn",
    )(x, g.reshape(1, d), w_in, w_in, w_out)


def _linear_kernel(*refs, mode, has_res):
    it = iter(refs)
    x_ref = next(it)
    p_ref = next(it) if mode in ("norm", "mul") else None
    w_ref = next(it)
    res_ref = next(it) if has_res else None
    o_ref = next(it)
    xs_ref = next(it)

    @pl.when(pl.program_id(1) == 0)
    def _():
        x = x_ref[...]
        if mode == "norm":
            x = _rms(x, p_ref[...])
        elif mode == "mul":
            x = x * p_ref[...]
        xs_ref[...] = x.astype(BF16)

    o = jnp.dot(xs_ref[...], w_ref[...], preferred_element_type=F32)
    if has_res:
        o = res_ref[...] + o
    o_ref[...] = o


def _linear(x, w, *, norm_g=None, mul=None, res=None, tm_target=512, tn_target=1024, name="linear"):
    n, kd = x.shape
    nout = w.shape[1]
    tm = _pick_tile(n, tm_target)
    tn = _pick_tile(nout, tn_target)
    mode = "norm" if norm_g is not None else ("mul" if mul is not None else "plain")
    args = [x]
    specs = [pl.BlockSpec((tm, kd), lambda i, j: (i, 0))]
    if mode == "norm":
        args.append(norm_g.reshape(1, kd))
        specs.append(pl.BlockSpec((1, kd), lambda i, j: (0, 0)))
    elif mode == "mul":
        args.append(mul)
        specs.append(pl.BlockSpec((tm, kd), lambda i, j: (i, 0)))
    args.append(w)
    specs.append(pl.BlockSpec((kd, tn), lambda i, j: (0, j)))
    if res is not None:
        args.append(res)
        specs.append(pl.BlockSpec((tm, tn), lambda i, j: (i, j)))
    return pl.pallas_call(
        functools.partial(_linear_kernel, mode=mode, has_res=res is not None),
        grid=(n // tm, nout // tn),
        in_specs=specs,
        out_specs=pl.BlockSpec((tm, tn), lambda i, j: (i, j)),
        out_shape=jax.ShapeDtypeStruct((n, nout), F32),
        scratch_shapes=[pltpu.VMEM((tm, kd), BF16)],
        compiler_params=_cparams(("parallel", "arbitrary")),
        name=name,
    )(*args)


def _rmsnorm_kernel(x_ref, g_ref, o_ref):
    o_ref[...] = _rms(x_ref[...], g_ref[...])


def _rmsnorm(x, g, *, tm_target=1024):
    n, d = x.shape
    tm = _pick_tile(n, tm_target)
    return pl.pallas_call(
        _rmsnorm_kernel,
        grid=(n // tm,),
        in_specs=[pl.BlockSpec((tm, d), lambda i: (i, 0)), pl.BlockSpec((1, d), lambda i: (0, 0))],
        out_specs=pl.BlockSpec((tm, d), lambda i: (i, 0)),
        out_shape=jax.ShapeDtypeStruct((n, d), F32),
        compiler_params=_cparams(("parallel",)),
        name="final_norm",
    )(x, g.reshape(1, d))


def _xa_prompt_kernel(x_ref, g_ref, wq_ref, kv_ref, wo_ref, o_ref):
    x = x_ref[...]
    q = jnp.dot(_rms(x, g_ref[...]).astype(BF16), wq_ref[...], preferred_element_type=F32)
    kv = kv_ref[...].astype(BF16)
    outs = []
    for h in range(XA_HEADS):
        lo = h * XA_HEAD_DIM
        s = _bdot_nt(q[:, lo:lo + XA_HEAD_DIM], kv[:, lo:lo + XA_HEAD_DIM]) * (XA_HEAD_DIM ** -0.5)
        m = jnp.max(s, axis=-1, keepdims=True)
        p = jnp.exp(s - m)
        l = jnp.sum(p, axis=-1, keepdims=True)
        o = jnp.dot(p.astype(BF16), kv[:, XA_WIDTH + lo:XA_WIDTH + lo + XA_HEAD_DIM],
                    preferred_element_type=F32)
        outs.append(o / l)
    o = jnp.concatenate(outs, axis=-1)
    o_ref[...] = x + jnp.dot(o.astype(BF16), wo_ref[...], preferred_element_type=F32)


def _xa_prompt(x, g, wq, kv, wo, *, seq, tm_target=512):
    n, d = x.shape
    n_mem = kv.shape[1]
    tm = _pick_tile(seq, tm_target)
    per_b = seq // tm
    return pl.pallas_call(
        _xa_prompt_kernel,
        grid=(n // tm,),
        in_specs=[
            pl.BlockSpec((tm, d), lambda i: (i, 0)),
            pl.BlockSpec((1, d), lambda i: (0, 0)),
            pl.BlockSpec((d, XA_WIDTH), lambda i: (0, 0)),
            pl.BlockSpec((None, n_mem, 2 * XA_WIDTH), lambda i: (i // per_b, 0, 0)),
            pl.BlockSpec((XA_WIDTH, d), lambda i: (0, 0)),
        ],
        out_specs=pl.BlockSpec((tm, d), lambda i: (i, 0)),
        out_shape=jax.ShapeDtypeStruct((n, d), F32),
        compiler_params=_cparams(("parallel",)),
        name="xa_prompt",
    )(x, g.reshape(1, d), wq, kv, wo)


def _xa_sample_kernel(q_ref, kv_ref, o_ref):
    q = q_ref[...]
    k = kv_ref[:, :, 0]
    v = kv_ref[:, :, 1]
    s = jnp.sum(k * q[:, None], axis=-1, keepdims=True) * (XA_HEAD_DIM ** -0.5)
    m = jnp.max(s, axis=1, keepdims=True)
    p = jnp.exp(s - m)
    l = jnp.sum(p, axis=1)
    o_ref[...] = jnp.sum(p * v, axis=1) / l


def _xa_sample(q, kv, layer, *, bb=8):
    b = q.shape[0]
    n_mem = kv.shape[2]
    bb = _pick_tile(b, bb)
    hd = (XA_HEADS, XA_HEAD_DIM)
    out = pl.pallas_call(
        _xa_sample_kernel,
        grid=(b // bb,),
        in_specs=[
            pl.BlockSpec((bb,) + hd, lambda i: (i, 0, 0)),
            pl.BlockSpec((None, bb, n_mem, 2) + hd, lambda i: (layer, i, 0, 0, 0, 0)),
        ],
        out_specs=pl.BlockSpec((bb,) + hd, lambda i: (i, 0, 0)),
        out_shape=jax.ShapeDtypeStruct((b,) + hd, F32),
        compiler_params=_cparams(("parallel",)),
        name="xa_sample",
    )(q.reshape((b,) + hd), kv)
    return out.reshape(b, XA_WIDTH)


def _band_kernel(q_ref, kp_ref, kc_ref, vp_ref, vc_ref, o_ref, l_ref, *, dil, group):
    n = pl.program_id(2)
    q = (q_ref[...] * (SWA_HEAD_DIM ** -0.5)).astype(BF16)
    k = jnp.concatenate([kp_ref[...], kc_ref[...]], axis=0).astype(BF16)
    v = jnp.concatenate([vp_ref[...], vc_ref[...]], axis=0).astype(BF16)
    qi = lax.broadcasted_iota(jnp.int32, (SWA_BLOCK, 2 * SWA_BLOCK), 0)
    ki = lax.broadcasted_iota(jnp.int32, (SWA_BLOCK, 2 * SWA_BLOCK), 1)
    delta = qi + SWA_BLOCK - ki
    units = SWA_WINDOWS[group] // dil
    k_min = jnp.where(n > 0, 0, SWA_BLOCK)
    valid = (delta >= 0) & (delta <= units) & (ki >= k_min)
    dist = (dil * delta).astype(F32)
    for h in range(SWA_HEADS):
        lo = h * SWA_HEAD_DIM
        slope = 2.0 ** (-8.0 * (group * SWA_HEADS + h + 1) / (N_GROUPS * SWA_HEADS))
        s = _bdot_nt(q[:, lo:lo + SWA_HEAD_DIM], k[:, lo:lo + SWA_HEAD_DIM])
        s = jnp.where(valid, s - slope * dist, NEG_BIG)
        m = jnp.max(s, axis=-1, keepdims=True)
        p = jnp.exp(s - m)
        l = jnp.sum(p, axis=-1, keepdims=True)
        o = jnp.dot(p.astype(BF16), v[:, lo:lo + SWA_HEAD_DIM], preferred_element_type=F32) / l
        o_ref[:, lo:lo + SWA_HEAD_DIM] = o
        l_ref[:, lo:lo + SWA_HEAD_DIM] = jnp.broadcast_to(m + jnp.log(l), (SWA_BLOCK, SWA_HEAD_DIM))


def _swa_qkv_kernel(x_ref, g_ref, w_ref, o0_ref, o1_ref, o2_ref, tmp_ref):
    xs = _rms(x_ref[...], g_ref[...]).astype(BF16)
    tm = xs.shape[0]
    gw = 3 * SWA_WIDTH
    for g, o_ref in enumerate((o0_ref, o1_ref, o2_ref)):
        dil = SWA_DILATIONS[g]
        out = jnp.dot(xs, w_ref[:, g * gw:(g + 1) * gw], preferred_element_type=F32)
        if dil == 1:
            o_ref[0] = out
        else:
            for cb in range(gw // 128):
                tmp_ref[cb] = out[:, cb * 128:(cb + 1) * 128]
            for r in range(dil):
                for cb in range(gw // 128):
                    o_ref[r, :, cb * 128:(cb + 1) * 128] = tmp_ref[cb, pl.ds(r, tm // dil, stride=dil), :]


def _swa_qkv_prompt(x, g, w, *, bsz, seq, tm_target=256):
    n, d = x.shape
    tm = _pick_tile(seq, tm_target)
    per_b = seq // tm
    gw = 3 * SWA_WIDTH
    out_specs, out_shapes = [], []
    for dil in SWA_DILATIONS:
        out_specs.append(pl.BlockSpec((None, dil, tm // dil, gw), lambda i: (i // per_b, 0, i % per_b, 0)))
        out_shapes.append(jax.ShapeDtypeStruct((bsz, dil, seq // dil, gw), F32))
    return pl.pallas_call(
        _swa_qkv_kernel,
        grid=(n // tm,),
        in_specs=[pl.BlockSpec((tm, d), lambda i: (i, 0)),
                  pl.BlockSpec((1, d), lambda i: (0, 0)),
                  pl.BlockSpec(w.shape, lambda i: (0, 0))],
        out_specs=out_specs,
        out_shape=out_shapes,
        scratch_shapes=[pltpu.VMEM((gw // 128, tm, 128), F32)],
        compiler_params=_cparams(("parallel",)),
        name="swa_qkv_prompt",
    )(x, g.reshape(1, d), w)


def _band_attention(qkv_g, group):
    bsz, dil, l_sub, _ = qkv_g.shape
    nb = l_sub // SWA_BLOCK
    blk = (None, None, SWA_BLOCK, SWA_WIDTH)
    cur = lambda col: pl.BlockSpec(blk, lambda b, r, n: (b, r, n, col))
    prev = lambda col: pl.BlockSpec(blk, lambda b, r, n: (b, r, jnp.maximum(n - 1, 0), col))
    out_spec = pl.BlockSpec(blk, lambda b, r, n: (b, r, n, 0))
    return pl.pallas_call(
        functools.partial(_band_kernel, dil=dil, group=group),
        grid=(bsz, dil, nb),
        in_specs=[cur(0), prev(1), cur(1), prev(2), cur(2)],
        out_specs=[out_spec, out_spec],
        out_shape=[jax.ShapeDtypeStruct((bsz, dil, l_sub, SWA_WIDTH), F32)] * 2,
        compiler_params=_cparams(("parallel", "parallel", "arbitrary")),
        name="swa_band_g%d" % group,
    )(qkv_g, qkv_g, qkv_g, qkv_g, qkv_g)


def _merge_kernel(o0, o1, o2, l0, l1, l2, x_ref, w_ref, out_ref, *tmp_refs):
    tm = x_ref.shape[0]
    vals = [o0[0], l0[0]]
    tmp = iter(tmp_refs)
    for g, (o_ref, l_ref) in ((1, (o1, l1)), (2, (o2, l2))):
        dil = SWA_DILATIONS[g]
        for ref in (o_ref, l_ref):
            t_ref = next(tmp)
            for r in range(dil):
                for cb in range(SWA_WIDTH // 128):
                    t_ref[cb, pl.ds(r, tm // dil, stride=dil), :] = ref[r, :, cb * 128:(cb + 1) * 128]
            vals.append(jnp.concatenate([t_ref[cb] for cb in range(SWA_WIDTH // 128)], axis=-1))
    o_0, a0, o_1, a1, o_2, a2 = vals
    m = jnp.maximum(jnp.maximum(a0, a1), a2)
    e0, e1, e2 = jnp.exp(a0 - m), jnp.exp(a1 - m), jnp.exp(a2 - m)
    y = (e0 * o_0 + e1 * o_1 + e2 * o_2) / (e0 + e1 + e2)
    out_ref[...] = x_ref[...] + jnp.dot(y.astype(BF16), w_ref[...], preferred_element_type=F32)


def _swa_merge(outs, lses, x, wo, *, seq, tm_target=256):
    n, d = x.shape
    tm = _pick_tile(seq, tm_target)
    per_b = seq // tm
    specs = [pl.BlockSpec((None, dil, tm // dil, SWA_WIDTH), lambda i: (i // per_b, 0, i % per_b, 0))
             for dil in SWA_DILATIONS]
    return pl.pallas_call(
        _merge_kernel,
        grid=(n // tm,),
        in_specs=specs + specs + [pl.BlockSpec((tm, d), lambda i: (i, 0)),
                                  pl.BlockSpec((SWA_WIDTH, d), lambda i: (0, 0))],
        out_specs=pl.BlockSpec((tm, d), lambda i: (i, 0)),
        out_shape=jax.ShapeDtypeStruct((n, d), F32),
        scratch_shapes=[pltpu.VMEM((SWA_WIDTH // 128, tm, 128), F32)] * 4,
        compiler_params=_cparams(("parallel",)),
        name="swa_merge",
    )(*outs, *lses, x, wo)


def _shift_attend_kernel(*refs, group, has_prev):
    it = iter(refs)
    c_ref, new_ref = next(it), next(it)
    if has_prev:
        next(it)
    co_ref, o_ref, l_ref = next(it), next(it), next(it)
    dil = SWA_DILATIONS[group]
    w = c_ref.shape[-1]
    pos = lax.broadcasted_iota(jnp.int32, (1, 1, w), 2)
    valid = (pos % dil) == 0
    dist = (w - pos).astype(F32)
    is_newest = pos == w - 1
    outs, lses = [], []
    for h in range(SWA_HEADS):
        slope = 2.0 ** (-8.0 * (group * SWA_HEADS + h + 1) / (N_GROUPS * SWA_HEADS))
        q_col = new_ref[:, 0, :, h:h + 1] * (SWA_HEAD_DIM ** -0.5)
        k_col = new_ref[:, 1, :, h:h + 1]
        v_col = new_ref[:, 2, :, h:h + 1]
        k_h = c_ref[:, 0, h]
        v_h = c_ref[:, 1, h]
        co_ref[:, 0, h] = jnp.where(is_newest, k_col, pltpu.roll(k_h, w - 1, axis=2))
        co_ref[:, 1, h] = jnp.where(is_newest, v_col, pltpu.roll(v_h, w - 1, axis=2))
        s = jnp.sum(k_h * q_col, axis=1, keepdims=True) - slope * dist
        s = jnp.where(valid, s, NEG_BIG)
        s0 = jnp.sum(k_col * q_col, axis=1, keepdims=True)
        m = jnp.maximum(jnp.max(s, axis=-1, keepdims=True), s0)
        p = jnp.exp(s - m)
        p0 = jnp.exp(s0 - m)
        l = jnp.sum(p, axis=-1, keepdims=True) + p0
        outs.append((jnp.sum(v_h * p, axis=-1, keepdims=True) + p0 * v_col) / l)
        lses.append(m + jnp.log(l))
    o_ref[...] = jnp.concatenate(outs, axis=-1)
    l_ref[...] = jnp.concatenate(lses, axis=-1)


def _shift_attend(cache_t, new_t, prev_out, *, layer, group, block_bytes=8 * 1024 * 1024):
    n_att, b = cache_t.shape[:2]
    w = cache_t.shape[-1]
    bb = _pick_tile(b, max(1, block_bytes // (2 * SWA_WIDTH * w * 4)))
    blk = (None, bb, 2, SWA_HEADS, SWA_HEAD_DIM, w)
    in_specs = [pl.BlockSpec(blk, lambda i: (layer, i, 0, 0, 0, 0)),
                pl.BlockSpec((bb, 3, SWA_HEAD_DIM, SWA_HEADS), lambda i: (i, 0, 0, 0))]
    args = [cache_t, new_t]
    aliases = {}
    if prev_out is not None:
        in_specs.append(pl.BlockSpec(memory_space=pl.ANY))
        args.append(prev_out)
        aliases = {2: 0}
    return pl.pallas_call(
        functools.partial(_shift_attend_kernel, group=group, has_prev=prev_out is not None),
        grid=(b // bb,),
        in_specs=in_specs,
        out_specs=[pl.BlockSpec(blk, lambda i: (layer, i, 0, 0, 0, 0)),
                   pl.BlockSpec((bb, SWA_HEAD_DIM, SWA_HEADS), lambda i: (i, 0, 0)),
                   pl.BlockSpec((bb, 1, SWA_HEADS), lambda i: (i, 0, 0))],
        out_shape=[jax.ShapeDtypeStruct(cache_t.shape, cache_t.dtype),
                   jax.ShapeDtypeStruct((b, SWA_HEAD_DIM, SWA_HEADS), F32),
                   jax.ShapeDtypeStruct((b, 1, SWA_HEADS), F32)],
        input_output_aliases=aliases,
        compiler_params=_cparams(("parallel",)),
        name="swa_shift_attend_g%d" % group,
    )(*args)


def _group_merge_kernel(o0, o1, o2, l0, l1, l2, y_ref):
    a0, a1, a2 = l0[...], l1[...], l2[...]
    m = jnp.maximum(jnp.maximum(a0, a1), a2)
    e0, e1, e2 = jnp.exp(a0 - m), jnp.exp(a1 - m), jnp.exp(a2 - m)
    y_ref[...] = (e0 * o0[...] + e1 * o1[...] + e2 * o2[...]) / (e0 + e1 + e2)


def _group_merge(outs, lses):
    return pl.pallas_call(
        _group_merge_kernel,
        out_shape=jax.ShapeDtypeStruct(outs[0].shape, F32),
        name="swa_group_merge",
    )(*outs, *lses)


def _rwkv_proj_kernel(*refs, seq_mode, has_vres, tiles_per_seq):
    it = iter(refs)
    x_ref = next(it)
    prev_ref = next(it)
    vfirst_ref = next(it) if has_vres else None
    g_ref, mix_ref, wr_ref, wk_ref, wv_ref = next(it), next(it), next(it), next(it), next(it)
    w0_ref, w1_ref, w2_ref = next(it), next(it), next(it)
    a0_ref, a1_ref, a2_ref = next(it), next(it), next(it)
    if has_vres:
        v0_ref, v1_ref, v2_ref = next(it), next(it), next(it)
    g1_ref, g2_ref, kk_ref, ka_ref = next(it), next(it), next(it), next(it)
    hsel_ref, hselt_ref = next(it), next(it)
    r_out, ld_out, k_out, v_out, a_out, b_out, g_out, xn_last_out = (next(it) for _ in range(8))

    xn = _rms(x_ref[...], g_ref[...])
    tm = xn.shape[0]
    if seq_mode:
        i = pl.program_id(0)
        pr = _rms(prev_ref[...], g_ref[...])[7:8]
        pr = jnp.where(i % tiles_per_seq == 0, 0.0, pr)
        row = lax.broadcasted_iota(jnp.int32, xn.shape, 0)
        xp = jnp.where(row == 0, pr, pltpu.roll(xn, 1, axis=0))
    else:
        xp = prev_ref[...]
    xx = xp - xn
    mix = mix_ref[...]
    xr, xw, xk, xv, xa, xg = (xn + xx * mix[j:j + 1] for j in range(6))
    r = _bdot(xr, wr_ref[...])
    k = _bdot(xk, wk_ref[...])
    v = _bdot(xv, wv_ref[...])
    z = w0_ref[...] + _bdot(jnp.tanh(_bdot(xw, w1_ref[...])), w2_ref[...])
    w_log = -(jnp.maximum(-z, 0.0) + jnp.log(1.0 + jnp.exp(-jnp.abs(z)))) - 0.5
    ld = -jnp.exp(w_log)
    a = _sigmoid(a0_ref[...] + _bdot(_bdot(xa, a1_ref[...]), a2_ref[...]))
    g = _bdot(_sigmoid(_bdot(xg, g1_ref[...])), g2_ref[...])
    if has_vres:
        v = v + (vfirst_ref[...] - v) * _sigmoid(v0_ref[...] + _bdot(_bdot(xv, v1_ref[...]), v2_ref[...]))
    kk = k * kk_ref[...]
    ssq = jnp.dot((kk * kk).astype(BF16), hsel_ref[...], preferred_element_type=F32)
    rs = lax.rsqrt(jnp.maximum(ssq, 1e-24))
    kk = kk * _dot_sel(rs, hselt_ref[...])
    k = k * (1.0 + (a - 1.0) * ka_ref[...])
    r_out[...] = r
    ld_out[...] = ld
    k_out[...] = k
    v_out[...] = v
    a_out[...] = -kk
    b_out[...] = kk * a
    g_out[...] = g
    xn_last_out[...] = xn[tm - 8:tm] if seq_mode else xn


def _head_selectors(d):
    heads = d // RW_HEAD
    ch = jnp.arange(d) // RW_HEAD
    hsel = (ch[:, None] == jnp.arange(128)[None, :]).astype(BF16)
    return hsel, hsel.T


def _rwkv_proj(x, prev, v_first, p, *, seq_mode, seq, tm_target=256):
    n, d = x.shape
    tm = _pick_tile(seq if seq_mode else n, tm_target)
    has_vres = v_first is not None
    row = lambda a: a.reshape(1, -1)
    full = lambda a: pl.BlockSpec(a.shape, lambda i: (0,) * a.ndim)
    tile = pl.BlockSpec((tm, d), lambda i: (i, 0))
    args, specs = [x], [tile]
    if seq_mode:
        args.append(x)
        specs.append(pl.BlockSpec((8, d), lambda i: (jnp.maximum(i * (tm // 8) - 1, 0), 0)))
    else:
        args.append(prev)
        specs.append(tile)
    if has_vres:
        args.append(v_first)
        specs.append(tile)
    hsel, hselt = _head_selectors(d)
    consts = [row(p["norm_g"]), p["mix"], p["w_r"], p["w_k"], p["w_v"],
              row(p["w0"]), p["w1"], p["w2"], row(p["a0"]), p["a1"], p["a2"]]
    if has_vres:
        consts += [row(p["v0"]), p["v1"], p["v2"]]
    consts += [p["g1"], p["g2"], row(p["k_k"]), row(p["k_a"]), hsel, hselt]
    args += consts
    specs += [full(c) for c in consts]
    outs = pl.pallas_call(
        functools.partial(_rwkv_proj_kernel, seq_mode=seq_mode, has_vres=has_vres,
                          tiles_per_seq=(seq // tm) if seq_mode else 1),
        grid=(n // tm,),
        in_specs=specs,
        out_specs=[tile] * 7 + [pl.BlockSpec((8, d), lambda i: (i, 0)) if seq_mode else tile],
        out_shape=[jax.ShapeDtypeStruct((n, d), F32)] * 7
        + [jax.ShapeDtypeStruct((n // tm * 8 if seq_mode else n, d), F32)],
        compiler_params=_cparams(("parallel",)),
        name="rwkv_proj",
    )(*args)
    return outs


def _wkv_chunk_kernel(r_ref, ld_ref, k_ref, v_ref, a_ref, b_ref, lnw_ref, lnb_ref, rk_ref,
                      y_ref, s_out_ref, s_ref, pre_ref, mat_ref, gam_ref, *, n_chunks):
    C = WKV_CHUNK
    HD = RW_HEAD
    P = 2 * RW_HEAD
    t = pl.program_id(2)
    slot_wr = t % 2
    slot_rd = 1 - slot_wr

    @pl.when(t == 0)
    def _():
        s_ref[...] = jnp.zeros_like(s_ref)
        pre_ref[1] = jnp.zeros(pre_ref.shape[1:], F32)
        mat_ref[1] = jnp.zeros(mat_ref.shape[1:], F32)
        gam_ref[1] = jnp.zeros(gam_ref.shape[1:], F32)

    ri = lax.broadcasted_iota(jnp.int32, (C, C), 0)
    ci = lax.broadcasted_iota(jnp.int32, (C, C), 1)
    tril = ri >= ci
    stril = ri > ci
    tri_ones = tril.astype(BF16)
    first = lambda rows: lax.broadcasted_iota(jnp.int32, (rows, P), 1) < HD
    first_h, first_c, first_2c = first(HD), first(C), first(2 * C)
    block_ones = (lax.broadcasted_iota(jnp.int32, (P, P), 0) // HD
                  == lax.broadcasted_iota(jnp.int32, (P, P), 1) // HD).astype(BF16)
    lnw, lnb, rk = lnw_ref[...], lnb_ref[...], rk_ref[...]

    chunks = []
    chains = []
    for c in range(n_chunks):
        sl = slice(c * C, (c + 1) * C)
        rc, ldc, kc, vc, ac, bc = (z[sl, :] for z in (r_ref, ld_ref, k_ref, v_ref, a_ref, b_ref))
        cum = _sel_dot(tri_ones, ldc)
        e_pos = jnp.exp(cum)
        e_neg = jnp.exp(-cum)
        at = ac * jnp.exp(cum - ldc)
        rt = rc * e_pos
        bt = bc * e_neg
        kt = kc * e_neg
        last = cum[C - 1:C]
        e_last = jnp.exp(last - cum)
        bh = bc * e_last
        kh = kc * e_last
        gamma = jnp.exp(last)
        pre_ref[slot_wr, c, 2] = _bdot(rc * kc * rk, block_ones) * vc
        gam_ref[slot_wr, c] = jnp.broadcast_to(gamma, (8, P))
        chunk = dict(at_swapped=pltpu.roll(at, HD, axis=1), rt=rt, v=vc,
                     ar=jnp.concatenate([at, rt], axis=0), bk=jnp.concatenate([bt, kt], axis=0),
                     bkh=jnp.concatenate([bh, kh], axis=0))
        chunks.append(chunk)
        chains += [dict(h=0, chunk=chunk), dict(h=1, chunk=chunk)]

    def stage_scores():
        for ch in chains:
            ck = ch["chunk"]
            mine = first_2c if ch["h"] == 0 else ~first_2c
            sc = _bdot_nt(jnp.where(mine, ck["ar"], 0.0), ck["bk"])
            ch["lp"] = jnp.where(stril, sc[:C, :C], 0.0)
            ch["ak"] = jnp.where(stril, sc[:C, C:], 0.0)
            ch["rbk"] = jnp.concatenate([jnp.where(tril, sc[C:, :C], 0.0),
                                         jnp.where(tril, sc[C:, C:], 0.0)], axis=1)

    def stage_x0():
        for ch in chains:
            ck = ch["chunk"]
            akv = _bdot(ch["ak"], ck["v"])
            ch["x"] = (jnp.where(first_c, akv, ck["at_swapped"]) if ch["h"] == 0
                       else jnp.where(first_c, ck["at_swapped"], akv))

    def stage_level(square):
        nxt = [ch["x"] + _bdot(ch["lp"], ch["x"]) for ch in chains]
        if square:
            for ch in chains:
                ch["lp"] = _bdot(ch["lp"], ch["lp"])
        for ch, x in zip(chains, nxt):
            ch["x"] = x

    def stage_final_dots():
        for ch in chains:
            ck = ch["chunk"]
            vz = jnp.where(first_c, ck["v"], 0.0) if ch["h"] == 0 else jnp.where(first_c, 0.0, ck["v"])
            z = jnp.concatenate([ch["x"], vz], axis=0)
            ch["ry"] = _bdot(ch["rbk"], z)
            ch["qh"] = _bdot_tn(z, ck["bkh"])

    def stage_final_stores():
        for c, ck in enumerate(chunks):
            ry_a, ry_b = chains[2 * c]["ry"], chains[2 * c + 1]["ry"]
            qa, qb = chains[2 * c]["qh"], chains[2 * c + 1]["qh"]
            pre_ref[slot_wr, c, 0] = ck["rt"] + pltpu.roll(jnp.where(first_c, ry_b, ry_a), HD, axis=1)
            pre_ref[slot_wr, c, 1] = jnp.where(first_c, ry_a, ry_b)
            mat_ref[slot_wr, c, 0] = jnp.concatenate([jnp.where(first_h, qa[HD:], 0.0),
                                                      jnp.where(first_h, 0.0, qb[:HD])], axis=0)
            mat_ref[slot_wr, c, 1] = jnp.concatenate([jnp.where(first_h, qa[:HD], 0.0),
                                                      jnp.where(first_h, 0.0, qb[HD:])], axis=0)

    state = [s_ref[...]]

    def state_step(c):
        s0 = state[0]
        y = _bdot_nt(pre_ref[slot_rd, c, 0], s0) + pre_ref[slot_rd, c, 1]
        state[0] = s0 * gam_ref[slot_rd, c, 0:1] + _bdot(s0, mat_ref[slot_rd, c, 0]) + mat_ref[slot_rd, c, 1]
        mu = _bdot(y, block_ones) * (1.0 / HD)
        yc = y - mu
        var = _bdot(yc * yc, block_ones) * (1.0 / HD)
        y_ref[c * C:(c + 1) * C, :] = yc * lax.rsqrt(var + RW_LN_EPS) * lnw + lnb + pre_ref[slot_rd, c, 2]

    stages = [stage_scores, stage_x0]
    n = 1
    while n < C:
        stages.append(functools.partial(stage_level, 2 * n < C))
        n *= 2
    stages += [stage_final_dots, stage_final_stores]
    for si, stage in enumerate(stages):
        stage()
        for c in range(si * n_chunks // len(stages), (si + 1) * n_chunks // len(stages)):
            state_step(c)
    s_ref[...] = state[0]

    @pl.when(t == pl.num_programs(2) - 1)
    def _():
        s_out_ref[0] = s_ref[0:HD, 0:HD]
        s_out_ref[1] = s_ref[HD:P, HD:P]


def _wkv_chunked(r, ld, k, v, a, b, ln_w, ln_b, r_k, *, bsz, seq, tb_target=512):
    n, d = r.shape
    heads = d // RW_HEAD
    pairs = heads // 2
    tb = _pick_tile(seq, tb_target)
    per_seq = seq // tb
    n_chunks = tb // WKV_CHUNK
    in_tile = pl.BlockSpec((tb, 128), lambda bi, hp, t: (bi * per_seq + jnp.minimum(t, per_seq - 1), hp))
    out_tile = pl.BlockSpec((tb, 128), lambda bi, hp, t: (bi * per_seq + jnp.maximum(t - 1, 0), hp))
    vec = pl.BlockSpec((1, 128), lambda bi, hp, t: (0, hp))
    y, st = pl.pallas_call(
        functools.partial(_wkv_chunk_kernel, n_chunks=n_chunks),
        grid=(bsz, pairs, per_seq + 1),
        in_specs=[in_tile] * 6 + [vec] * 3,
        out_specs=[out_tile, pl.BlockSpec((None, 2, RW_HEAD, RW_HEAD), lambda bi, hp, t: (bi, hp, 0, 0))],
        out_shape=[jax.ShapeDtypeStruct((n, d), F32),
                   jax.ShapeDtypeStruct((bsz, heads, RW_HEAD, RW_HEAD), F32)],
        scratch_shapes=[pltpu.VMEM((128, 128), F32),
                        pltpu.VMEM((2, n_chunks, 3, WKV_CHUNK, 128), F32),
                        pltpu.VMEM((2, n_chunks, 2, 128, 128), F32),
                        pltpu.VMEM((2, n_chunks, 8, 128), F32)],
        compiler_params=_cparams(("parallel", "parallel", "arbitrary")),
        name="wkv_chunked",
    )(r, ld, k, v, a, b, ln_w.reshape(1, d), ln_b.reshape(1, d), r_k.reshape(1, d))
    return y, st


def _wkv_step_kernel(s_ref, r_ref, ld_ref, k_ref, a_ref, b_ref, rk_ref, v_ref, lnw_ref, lnb_ref,
                     s_out, y_out):
    s = s_ref[...]
    r, k, a, b = r_ref[...], k_ref[...], a_ref[...], b_ref[...]
    w = jnp.exp(ld_ref[...])
    v = v_ref[...]
    sa = jnp.sum(s * a, axis=-1, keepdims=True)
    s_new = s * w + sa * b + v * k
    y = jnp.sum(s_new * r, axis=-1, keepdims=True)
    mu = jnp.mean(y, axis=-2, keepdims=True)
    yc = y - mu
    var = jnp.mean(yc * yc, axis=-2, keepdims=True)
    yn = yc * lax.rsqrt(var + RW_LN_EPS) * lnw_ref[...] + lnb_ref[...]
    bonus = jnp.sum(r * k * rk_ref[...], axis=-1, keepdims=True) * v
    s_out[...] = s_new
    y_out[...] = yn + bonus


def _wkv_step(state, r, ld, k, v, a, b, ln_w, ln_b, r_k, *, bb=8):
    bsz, heads = state.shape[:2]
    bb = _pick_tile(bsz, bb)
    rows = lambda z: z.reshape(bsz, heads, 1, RW_HEAD)
    cols = lambda z: z.reshape(bsz, heads, RW_HEAD, 1)
    st_spec = pl.BlockSpec((bb, heads, RW_HEAD, RW_HEAD), lambda i: (i, 0, 0, 0))
    row_spec = pl.BlockSpec((bb, heads, 1, RW_HEAD), lambda i: (i, 0, 0, 0))
    col_spec = pl.BlockSpec((bb, heads, RW_HEAD, 1), lambda i: (i, 0, 0, 0))
    prow_spec = pl.BlockSpec((1, heads, 1, RW_HEAD), lambda i: (0, 0, 0, 0))
    pcol_spec = pl.BlockSpec((1, heads, RW_HEAD, 1), lambda i: (0, 0, 0, 0))
    s_new, y = pl.pallas_call(
        _wkv_step_kernel,
        grid=(bsz // bb,),
        in_specs=[st_spec] + [row_spec] * 5 + [prow_spec, col_spec, pcol_spec, pcol_spec],
        out_specs=[st_spec, col_spec],
        out_shape=[jax.ShapeDtypeStruct(state.shape, F32),
                   jax.ShapeDtypeStruct((bsz, heads, RW_HEAD, 1), F32)],
        compiler_params=_cparams(("parallel",)),
        name="wkv_step",
    )(state, rows(r), rows(ld), rows(k), rows(a), rows(b), r_k.reshape(1, heads, 1, RW_HEAD),
      cols(v), ln_w.reshape(1, heads, RW_HEAD, 1), ln_b.reshape(1, heads, RW_HEAD, 1))
    return y.reshape(bsz, heads * RW_HEAD), s_new


def _run_layers(x, mem_kv, wkv0, shift0, swa_bufs, W, *, bsz, seq):
    is_prompt = swa_bufs is None
    depth = W["norm_g"].shape[0]
    new_wkv, new_shift = [], []
    new_kv = [[] for _ in range(N_GROUPS)]
    v_first = None
    for l in range(depth):
        ng = W["norm_g"][l]
        x = _ffn(x, ng[0], W["ffn_w_in"][l, 0], W["ffn_w_out"][l, 0])
        i = l // 2
        if l % 2 == 0:
            p = dict(norm_g=ng[1], mix=W["rw_mix"][i], w_r=W["rw_w_rkv"][i, 0], w_k=W["rw_w_rkv"][i, 1],
                     w_v=W["rw_w_rkv"][i, 2], w0=W["rw_w0"][i], w1=W["rw_w1"][i], w2=W["rw_w2"][i],
                     a0=W["rw_a0"][i], a1=W["rw_a1"][i], a2=W["rw_a2"][i], g1=W["rw_g1"][i], g2=W["rw_g2"][i],
                     k_k=W["rw_k_k"][i], k_a=W["rw_k_a"][i])
            if i > 0:
                p.update(v0=W["rw_v0"][i - 1], v1=W["rw_v1"][i - 1], v2=W["rw_v2"][i - 1])
            prev = None if is_prompt else shift0[i]
            r, ld, k, v, a, b, g, xn_last = _rwkv_proj(x, prev, v_first if i > 0 else None, p,
                                                      seq_mode=is_prompt, seq=seq)
            if i == 0:
                v_first = v
            if is_prompt:
                y, s_new = _wkv_chunked(r, ld, k, v, a, b, W["rw_ln_w"][i], W["rw_ln_b"][i],
                                        W["rw_r_k"][i].reshape(-1), bsz=bsz, seq=seq)
                tiles = xn_last.shape[0] // 8 // bsz
                shift_new = xn_last.reshape(bsz, tiles, 8, -1)[:, -1, -1]
            else:
                y, s_new = _wkv_step(wkv0[i], r, ld, k, v, a, b, W["rw_ln_w"][i], W["rw_ln_b"][i],
                                     W["rw_r_k"][i].reshape(-1))
                shift_new = xn_last
            new_wkv.append(s_new)
            new_shift.append(shift_new)
            x = _linear(y, W["rw_w_o"][i], mul=g, res=x, name="rwkv_out")
        else:
            outs, lses = [], []
            if is_prompt:
                qkv_groups = _swa_qkv_prompt(x, ng[1], W["swa_w_qkv"][i], bsz=bsz, seq=seq)
                for gi in range(N_GROUPS):
                    o, lse = _band_attention(qkv_groups[gi], gi)
                    outs.append(o)
                    lses.append(lse)
                    dil = SWA_DILATIONS[gi]
                    keep = min(SWA_WINDOWS[gi], seq)
                    kv_tail = qkv_groups[gi][:, :, (seq - keep) // dil:, SWA_WIDTH:]
                    new_kv[gi].append(jnp.swapaxes(kv_tail, 1, 2).reshape(
                        bsz, keep, 2, SWA_HEADS, SWA_HEAD_DIM))
                x = _swa_merge(outs, lses, x, W["swa_w_o"][i], seq=seq)
            else:
                qkv = _linear(x, W["swa_w_qkv"][i], norm_g=ng[1], tn_target=512, name="swa_qkv")
                qkv5 = qkv.reshape(bsz, N_GROUPS, 3, SWA_HEADS, SWA_HEAD_DIM)
                for gi in range(N_GROUPS):
                    prev_out = new_kv[gi][0] if new_kv[gi] else None
                    new_t = jnp.swapaxes(qkv5[:, gi], -1, -2)
                    cache_out, o, lse = _shift_attend(swa_bufs[gi], new_t, prev_out, layer=i, group=gi)
                    new_kv[gi] = [cache_out]
                    outs.append(o)
                    lses.append(lse)
                y = jnp.swapaxes(_group_merge(outs, lses), 1, 2).reshape(bsz, SWA_WIDTH)
                x = _linear(y, W["swa_w_o"][i], res=x, name="swa_out")
        if is_prompt:
            x = _xa_prompt(x, ng[2], W["xa_w_q"][l], mem_kv[l], W["xa_w_o"][l], seq=seq)
        else:
            q = _linear(x, W["xa_w_q"][l], norm_g=ng[2], name="xa_q")
            o = _xa_sample(q, mem_kv, l)
            x = _linear(o, W["xa_w_o"][l], res=x, name="xa_out")
        x = _ffn(x, ng[3], W["ffn_w_in"][l, 1], W["ffn_w_out"][l, 1])
    y = _rmsnorm(x, W["final_norm_g"])
    return y, new_wkv, new_shift, new_kv


def kernel(x_prompt, x_sample, cache_mem_kv, state_rwkv_wkv, state_rwkv_shift, cache_swa_kv_g0, cache_swa_kv_g1, cache_swa_kv_g2, mem_prompt, norm_g, mem_norm_g, final_norm_g, ffn_w_in, ffn_w_out, rw_mix, rw_w_rkv, rw_w_o, rw_w0, rw_w1, rw_w2, rw_a0, rw_a1, rw_a2, rw_v0, rw_v1, rw_v2, rw_g1, rw_g2, rw_k_k, rw_k_a, rw_r_k, rw_ln_w, rw_ln_b, swa_w_qkv, swa_w_o, xa_w_q, xa_w_kv, xa_w_o):
    bf = lambda w: w.astype(BF16)
    W = dict(norm_g=norm_g, final_norm_g=final_norm_g, ffn_w_in=bf(ffn_w_in), ffn_w_out=bf(ffn_w_out),
             rw_mix=rw_mix, rw_w_rkv=bf(rw_w_rkv), rw_w_o=bf(rw_w_o), rw_w0=rw_w0, rw_w1=bf(rw_w1),
             rw_w2=bf(rw_w2), rw_a0=rw_a0, rw_a1=bf(rw_a1), rw_a2=bf(rw_a2), rw_v0=rw_v0, rw_v1=bf(rw_v1),
             rw_v2=bf(rw_v2), rw_g1=bf(rw_g1), rw_g2=bf(rw_g2), rw_k_k=rw_k_k, rw_k_a=rw_k_a, rw_r_k=rw_r_k,
             rw_ln_w=rw_ln_w, rw_ln_b=rw_ln_b, swa_w_qkv=bf(swa_w_qkv), swa_w_o=bf(swa_w_o),
             xa_w_q=bf(xa_w_q), xa_w_o=bf(xa_w_o))
    depth = norm_g.shape[0]
    bsz, seq, d = x_prompt.shape
    n_mem = mem_prompt.shape[1]

    mem2 = mem_prompt.reshape(bsz * n_mem, d)
    xa_w_kv_b = bf(xa_w_kv)
    mem_kv_p = jnp.stack([_linear(mem2, xa_w_kv_b[l], norm_g=mem_norm_g[l], name="mem_kv")
                          for l in range(depth)])
    y_p, wkv_p, shift_p, kv_p = _run_layers(
        x_prompt.reshape(bsz * seq, d), mem_kv_p.reshape(depth, bsz, n_mem, 2 * XA_WIDTH),
        None, None, None, W, bsz=bsz, seq=seq)

    dbsz, dseq, _ = x_sample.shape
    assert dseq == 1
    caches = tuple(jnp.transpose(c, (0, 1, 3, 4, 5, 2))
                   for c in (cache_swa_kv_g0, cache_swa_kv_g1, cache_swa_kv_g2))
    y_s, wkv_s, shift_s, kv_s = _run_layers(
        x_sample.reshape(dbsz, d), cache_mem_kv,
        state_rwkv_wkv, state_rwkv_shift, caches, W, bsz=dbsz, seq=1)
    swa_s = tuple(jnp.transpose(kv_s[g][0], (0, 1, 5, 2, 3, 4)) for g in range(N_GROUPS))

    return (y_p.reshape(bsz, seq, d), y_s.reshape(dbsz, 1, d),
            mem_kv_p.reshape(depth, bsz, n_mem, 2, XA_HEADS, XA_HEAD_DIM),
            jnp.stack(wkv_p), jnp.stack(shift_p),
            jnp.stack(kv_p[0]), jnp.stack(kv_p[1]), jnp.stack(kv_p[2]),
            jnp.stack(wkv_s), jnp.stack(shift_s), swa_s[0], swa_s[1], swa_s[2])
```

```python
import functools

import jax
import jax.numpy as jnp
from jax import lax
from jax.experimental import pallas as pl
from jax.experimental.pallas import tpu as pltpu

F32 = jnp.float32
BF16 = jnp.bfloat16

NORM_EPS = 1e-6
RW_HEAD = 64
RW_LN_EPS = RW_HEAD * 1e-5
SWA_WINDOWS = (128, 512, 2048)
SWA_DILATIONS = (1, 4, 16)
N_GROUPS = 3
SWA_HEADS = 8
SWA_HEAD_DIM = 64
SWA_WIDTH = SWA_HEADS * SWA_HEAD_DIM
SWA_BLOCK = 128
XA_HEADS = 4
XA_HEAD_DIM = 128
XA_WIDTH = XA_HEADS * XA_HEAD_DIM
WKV_CHUNK = 64
NEG_BIG = -1e30
VMEM_LIMIT = 56 * 1024 * 1024


def _cparams(sem):
    return pltpu.CompilerParams(dimension_semantics=sem, vmem_limit_bytes=VMEM_LIMIT)


def _bdot(a, b):
    return jnp.dot(a.astype(BF16), b.astype(BF16), preferred_element_type=F32)


def _bdot_nt(a, b):
    return lax.dot_general(a.astype(BF16), b.astype(BF16), (((1,), (1,)), ((), ())),
                           preferred_element_type=F32)


def _bdot_tn(a, b):
    return lax.dot_general(a.astype(BF16), b.astype(BF16), (((0,), (0,)), ((), ())),
                           preferred_element_type=F32)


def _split3(x):
    h1 = x.astype(BF16)
    r1 = x - h1.astype(F32)
    h2 = r1.astype(BF16)
    h3 = (r1 - h2.astype(F32)).astype(BF16)
    return h1, h2, h3


def _sel_dot(sel_bf16, x):
    h1, h2, h3 = _split3(x)
    d = lambda h: jnp.dot(sel_bf16, h, preferred_element_type=F32)
    return d(h1) + d(h2) + d(h3)


def _dot_sel(x, sel_bf16):
    h1, h2, h3 = _split3(x)
    d = lambda h: jnp.dot(h, sel_bf16, preferred_element_type=F32)
    return d(h1) + d(h2) + d(h3)


def _rms(x, g):
    return x * lax.rsqrt(jnp.mean(x * x, axis=-1, keepdims=True) + NORM_EPS) * g


def _sigmoid(x):
    return 1.0 / (1.0 + jnp.exp(-x))


def _pick_tile(n, target):
    t = min(n, target)
    while n % t:
        t //= 2
    return t


def _ffn_kernel(*refs, final_norm):
    if final_norm:
        x_ref, g_ref, wg_ref, wu_ref, wo_ref, fg_ref, o_ref, xn_ref, acc_ref = refs
    else:
        x_ref, g_ref, wg_ref, wu_ref, wo_ref, o_ref, xn_ref, acc_ref = refs
    j = pl.program_id(1)

    @pl.when(j == 0)
    def _():
        xn_ref[...] = _rms(x_ref[...], g_ref[...]).astype(BF16)
        acc_ref[...] = jnp.zeros_like(acc_ref)

    xn = xn_ref[...]
    gate = jnp.dot(xn, wg_ref[...], preferred_element_type=F32)
    up = jnp.dot(xn, wu_ref[...], preferred_element_type=F32)
    h = (gate * _sigmoid(gate) * up).astype(BF16)
    acc_ref[...] += jnp.dot(h, wo_ref[...], preferred_element_type=F32)

    @pl.when(j == pl.num_programs(1) - 1)
    def _():
        y = x_ref[...] + 0.5 * acc_ref[...]
        o_ref[...] = _rms(y, fg_ref[...]) if final_norm else y


def _ffn(x, g, w_in, w_out, *, final_g=None, tm_target=512, tf_target=1408):
    n, d = x.shape
    d_ff = w_out.shape[0]
    tm = _pick_tile(n, tm_target)
    tf = tf_target if d_ff % tf_target == 0 else 128
    nf = d_ff // tf
    vec = pl.BlockSpec((1, d), lambda i, j: (0, 0))
    in_specs = [
        pl.BlockSpec((tm, d), lambda i, j: (i, 0)),
        vec,
        pl.BlockSpec((d, tf), lambda i, j: (0, j)),
        pl.BlockSpec((d, tf), lambda i, j: (0, j + nf)),
        pl.BlockSpec((tf, d), lambda i, j: (j, 0)),
    ]
    args = [x, g.reshape(1, d), w_in, w_in, w_out]
    if final_g is not None:
        in_specs.append(vec)
        args.append(final_g.reshape(1, d))
    return pl.pallas_call(
        functools.partial(_ffn_kernel, final_norm=final_g is not None),
        grid=(n // tm, nf),
        in_specs=in_specs,
        out_specs=pl.BlockSpec((tm, d), lambda i, j: (i, 0)),
        out_shape=jax.ShapeDtypeStruct((n, d), F32),
        scratch_shapes=[pltpu.VMEM((tm, d), BF16), pltpu.VMEM((tm, d), F32)],
        compiler_params=_cparams(("parallel", "arbitrary")),
        name="ffn",
    )(*args)


def _linear_kernel(*refs, mode, has_res):
    it = iter(refs)
    x_ref = next(it)
    p_ref = next(it) if mode in ("norm", "mul") else None
    w_ref = next(it)
    res_ref = next(it) if has_res else None
    o_ref = next(it)
    xs_ref = next(it)

    @pl.when(pl.program_id(1) == 0)
    def _():
        x = x_ref[...]
        if mode == "norm":
            x = _rms(x, p_ref[...])
        elif mode == "mul":
            x = x * p_ref[...]
        xs_ref[...] = x.astype(BF16)

    o = jnp.dot(xs_ref[...], w_ref[...], preferred_element_type=F32)
    if has_res:
        o = res_ref[...] + o
    o_ref[...] = o


def _linear(x, w, *, norm_g=None, mul=None, res=None, tm_target=512, tn_target=1024, name="linear"):
    n, kd = x.shape
    nout = w.shape[1]
    tm = _pick_tile(n, tm_target)
    tn = _pick_tile(nout, tn_target)
    mode = "norm" if norm_g is not None else ("mul" if mul is not None else "plain")
    args = [x]
    specs = [pl.BlockSpec((tm, kd), lambda i, j: (i, 0))]
    if mode == "norm":
        args.append(norm_g.reshape(1, kd))
        specs.append(pl.BlockSpec((1, kd), lambda i, j: (0, 0)))
    elif mode == "mul":
        args.append(mul)
        specs.append(pl.BlockSpec((tm, kd), lambda i, j: (i, 0)))
    args.append(w)
    specs.append(pl.BlockSpec((kd, tn), lambda i, j: (0, j)))
    if res is not None:
        args.append(res)
        specs.append(pl.BlockSpec((tm, tn), lambda i, j: (i, j)))
    return pl.pallas_call(
        functools.partial(_linear_kernel, mode=mode, has_res=res is not None),
        grid=(n // tm, nout // tn),
        in_specs=specs,
        out_specs=pl.BlockSpec((tm, tn), lambda i, j: (i, j)),
        out_shape=jax.ShapeDtypeStruct((n, nout), F32),
        scratch_shapes=[pltpu.VMEM((tm, kd), BF16)],
        compiler_params=_cparams(("parallel", "arbitrary")),
        name=name,
    )(*args)


def _xa_prompt_kernel(x_ref, g_ref, wq_ref, kv_ref, wo_ref, o_ref):
    x = x_ref[...]
    q = jnp.dot(_rms(x, g_ref[...]).astype(BF16), wq_ref[...], preferred_element_type=F32)
    kv = kv_ref[...].astype(BF16)
    outs = []
    for h in range(XA_HEADS):
        lo = h * XA_HEAD_DIM
        s = _bdot_nt(q[:, lo:lo + XA_HEAD_DIM], kv[:, lo:lo + XA_HEAD_DIM]) * (XA_HEAD_DIM ** -0.5)
        m = jnp.max(s, axis=-1, keepdims=True)
        p = jnp.exp(s - m)
        l = jnp.sum(p, axis=-1, keepdims=True)
        o = jnp.dot(p.astype(BF16), kv[:, XA_WIDTH + lo:XA_WIDTH + lo + XA_HEAD_DIM],
                    preferred_element_type=F32)
        outs.append(o / l)
    o = jnp.concatenate(outs, axis=-1)
    o_ref[...] = x + jnp.dot(o.astype(BF16), wo_ref[...], preferred_element_type=F32)


def _xa_prompt(x, g, wq, kv, wo, *, seq, tm_target=512):
    n, d = x.shape
    n_mem = kv.shape[1]
    tm = _pick_tile(seq, tm_target)
    per_b = seq // tm
    return pl.pallas_call(
        _xa_prompt_kernel,
        grid=(n // tm,),
        in_specs=[
            pl.BlockSpec((tm, d), lambda i: (i, 0)),
            pl.BlockSpec((1, d), lambda i: (0, 0)),
            pl.BlockSpec((d, XA_WIDTH), lambda i: (0, 0)),
            pl.BlockSpec((None, n_mem, 2 * XA_WIDTH), lambda i: (i // per_b, 0, 0)),
            pl.BlockSpec((XA_WIDTH, d), lambda i: (0, 0)),
        ],
        out_specs=pl.BlockSpec((tm, d), lambda i: (i, 0)),
        out_shape=jax.ShapeDtypeStruct((n, d), F32),
        compiler_params=_cparams(("parallel",)),
        name="xa_prompt",
    )(x, g.reshape(1, d), wq, kv, wo)


def _xa_sample_kernel(q_ref, kv_ref, o_ref):
    q = q_ref[...]
    k = kv_ref[:, :, 0]
    v = kv_ref[:, :, 1]
    s = jnp.sum(k * q[:, None], axis=-1, keepdims=True) * (XA_HEAD_DIM ** -0.5)
    m = jnp.max(s, axis=1, keepdims=True)
    p = jnp.exp(s - m)
    l = jnp.sum(p, axis=1)
    o_ref[...] = jnp.sum(p * v, axis=1) / l


def _xa_sample(q, kv, layer, *, bb=8):
    b = q.shape[0]
    n_mem = kv.shape[2]
    bb = _pick_tile(b, bb)
    hd = (XA_HEADS, XA_HEAD_DIM)
    out = pl.pallas_call(
        _xa_sample_kernel,
        grid=(b // bb,),
        in_specs=[
            pl.BlockSpec((bb,) + hd, lambda i: (i, 0, 0)),
            pl.BlockSpec((None, bb, n_mem, 2) + hd, lambda i: (layer, i, 0, 0, 0, 0)),
        ],
        out_specs=pl.BlockSpec((bb,) + hd, lambda i: (i, 0, 0)),
        out_shape=jax.ShapeDtypeStruct((b,) + hd, F32),
        compiler_params=_cparams(("parallel",)),
        name="xa_sample",
    )(q.reshape((b,) + hd), kv)
    return out.reshape(b, XA_WIDTH)


def _band_bias(group):
    dil = SWA_DILATIONS[group]
    units = SWA_WINDOWS[group] // dil
    qi = jnp.arange(SWA_BLOCK)[:, None]
    ki = jnp.arange(2 * SWA_BLOCK)[None, :]
    delta = qi + SWA_BLOCK - ki
    band = (delta >= 0) & (delta <= units)
    heads = jnp.arange(SWA_HEADS, dtype=F32)
    slopes = jnp.exp2(-8.0 * (group * SWA_HEADS + heads + 1.0) / (N_GROUPS * SWA_HEADS))
    alibi = -slopes[:, None, None] * (dil * delta).astype(F32)[None]
    later = jnp.where(band[None], alibi, NEG_BIG)
    first = jnp.where((band & (ki >= SWA_BLOCK))[None], alibi, NEG_BIG)
    return jnp.stack([first, later])


def _band_kernel(q_ref, kp_ref, kc_ref, vp_ref, vc_ref, bias_ref, o_ref, l_ref):
    q = (q_ref[...] * (SWA_HEAD_DIM ** -0.5)).astype(BF16)
    k = jnp.concatenate([kp_ref[...], kc_ref[...]], axis=0).astype(BF16)
    v = jnp.concatenate([vp_ref[...], vc_ref[...]], axis=0).astype(BF16)
    heads = [slice(h * SWA_HEAD_DIM, (h + 1) * SWA_HEAD_DIM) for h in range(SWA_HEADS)]
    ss = [_bdot_nt(q[:, hs], k[:, hs]) + bias_ref[h] for h, hs in enumerate(heads)]
    ms = [jnp.max(s, axis=-1, keepdims=True) for s in ss]
    ps = [jnp.exp(s - m) for s, m in zip(ss, ms)]
    ls = [jnp.sum(p, axis=-1, keepdims=True) for p in ps]
    os = [jnp.dot(p.astype(BF16), v[:, hs], preferred_element_type=F32) for p, hs in zip(ps, heads)]
    for hs, o, m, l in zip(heads, os, ms, ls):
        o_ref[:, hs] = o / l
        l_ref[:, hs] = jnp.broadcast_to(m + jnp.log(l), (SWA_BLOCK, SWA_HEAD_DIM))


def _swa_qkv_kernel(x_ref, g_ref, w_ref, o0_ref, o1_ref, o2_ref, tmp_ref):
    xs = _rms(x_ref[...], g_ref[...]).astype(BF16)
    tm = xs.shape[0]
    gw = 3 * SWA_WIDTH
    for g, o_ref in enumerate((o0_ref, o1_ref, o2_ref)):
        dil = SWA_DILATIONS[g]
        out = jnp.dot(xs, w_ref[:, g * gw:(g + 1) * gw], preferred_element_type=F32)
        if dil == 1:
            o_ref[0] = out
        else:
            for cb in range(gw // 128):
                tmp_ref[cb] = out[:, cb * 128:(cb + 1) * 128]
            for r in range(dil):
                for cb in range(gw // 128):
                    o_ref[r, :, cb * 128:(cb + 1) * 128] = tmp_ref[cb, pl.ds(r, tm // dil, stride=dil), :]


def _swa_qkv_prompt(x, g, w, *, bsz, seq, tm_target=512):
    n, d = x.shape
    tm = _pick_tile(seq, tm_target)
    per_b = seq // tm
    gw = 3 * SWA_WIDTH
    out_specs, out_shapes = [], []
    for dil in SWA_DILATIONS:
        out_specs.append(pl.BlockSpec((None, dil, tm // dil, gw), lambda i: (i // per_b, 0, i % per_b, 0)))
        out_shapes.append(jax.ShapeDtypeStruct((bsz, dil, seq // dil, gw), F32))
    return pl.pallas_call(
        _swa_qkv_kernel,
        grid=(n // tm,),
        in_specs=[pl.BlockSpec((tm, d), lambda i: (i, 0)),
                  pl.BlockSpec((1, d), lambda i: (0, 0)),
                  pl.BlockSpec(w.shape, lambda i: (0, 0), pipeline_mode=pl.Buffered(1))],
        out_specs=out_specs,
        out_shape=out_shapes,
        scratch_shapes=[pltpu.VMEM((gw // 128, tm, 128), F32)],
        compiler_params=_cparams(("parallel",)),
        name="swa_qkv_prompt",
    )(x, g.reshape(1, d), w)


def _band_attention(qkv_g, group):
    bsz, dil, l_sub, _ = qkv_g.shape
    nb = l_sub // SWA_BLOCK
    blk = (None, None, SWA_BLOCK, SWA_WIDTH)
    cur = lambda col: pl.BlockSpec(blk, lambda b, r, n: (b, r, n, col))
    prev = lambda col: pl.BlockSpec(blk, lambda b, r, n: (b, r, jnp.maximum(n - 1, 0), col))
    out_spec = pl.BlockSpec(blk, lambda b, r, n: (b, r, n, 0))
    bias_spec = pl.BlockSpec((None, SWA_HEADS, SWA_BLOCK, 2 * SWA_BLOCK),
                             lambda b, r, n: (jnp.minimum(n, 1), 0, 0, 0))
    return pl.pallas_call(
        _band_kernel,
        grid=(bsz, dil, nb),
        in_specs=[cur(0), prev(1), cur(1), prev(2), cur(2), bias_spec],
        out_specs=[out_spec, out_spec],
        out_shape=[jax.ShapeDtypeStruct((bsz, dil, l_sub, SWA_WIDTH), F32)] * 2,
        compiler_params=_cparams(("parallel", "parallel", "arbitrary")),
        name="swa_band_g%d" % group,
    )(qkv_g, qkv_g, qkv_g, qkv_g, qkv_g, _band_bias(group))


def _merge_kernel(o0, o1, o2, l0, l1, l2, x_ref, w_ref, out_ref, *tmp_refs):
    tm = x_ref.shape[0]
    vals = [o0[0], l0[0]]
    tmp = iter(tmp_refs)
    for g, (o_ref, l_ref) in ((1, (o1, l1)), (2, (o2, l2))):
        dil = SWA_DILATIONS[g]
        for ref in (o_ref, l_ref):
            t_ref = next(tmp)
            for r in range(dil):
                for cb in range(SWA_WIDTH // 128):
                    t_ref[cb, pl.ds(r, tm // dil, stride=dil), :] = ref[r, :, cb * 128:(cb + 1) * 128]
            vals.append(jnp.concatenate([t_ref[cb] for cb in range(SWA_WIDTH // 128)], axis=-1))
    o_0, a0, o_1, a1, o_2, a2 = vals
    m = jnp.maximum(jnp.maximum(a0, a1), a2)
    e0, e1, e2 = jnp.exp(a0 - m), jnp.exp(a1 - m), jnp.exp(a2 - m)
    y = (e0 * o_0 + e1 * o_1 + e2 * o_2) / (e0 + e1 + e2)
    out_ref[...] = x_ref[...] + jnp.dot(y.astype(BF16), w_ref[...], preferred_element_type=F32)


def _swa_merge(outs, lses, x, wo, *, seq, tm_target=256):
    n, d = x.shape
    tm = _pick_tile(seq, tm_target)
    per_b = seq // tm
    specs = [pl.BlockSpec((None, dil, tm // dil, SWA_WIDTH), lambda i: (i // per_b, 0, i % per_b, 0))
             for dil in SWA_DILATIONS]
    return pl.pallas_call(
        _merge_kernel,
        grid=(n // tm,),
        in_specs=specs + specs + [pl.BlockSpec((tm, d), lambda i: (i, 0)),
                                  pl.BlockSpec((SWA_WIDTH, d), lambda i: (0, 0))],
        out_specs=pl.BlockSpec((tm, d), lambda i: (i, 0)),
        out_shape=jax.ShapeDtypeStruct((n, d), F32),
        scratch_shapes=[pltpu.VMEM((SWA_WIDTH // 128, tm, 128), F32)] * 4,
        compiler_params=_cparams(("parallel",)),
        name="swa_merge",
    )(*outs, *lses, x, wo)


def _shift_attend_kernel(*refs, group, has_prev):
    it = iter(refs)
    c_ref, new_ref = next(it), next(it)
    if has_prev:
        next(it)
    co_ref, o_ref, l_ref = next(it), next(it), next(it)
    dil = SWA_DILATIONS[group]
    w = c_ref.shape[-1]
    pos = lax.broadcasted_iota(jnp.int32, (1, 1, w), 2)
    valid = (pos % dil) == 0
    dist = (w - pos).astype(F32)
    is_newest = pos == w - 1
    outs, lses = [], []
    for h in range(SWA_HEADS):
        slope = 2.0 ** (-8.0 * (group * SWA_HEADS + h + 1) / (N_GROUPS * SWA_HEADS))
        q_col = new_ref[:, 0, :, h:h + 1] * (SWA_HEAD_DIM ** -0.5)
        k_col = new_ref[:, 1, :, h:h + 1]
        v_col = new_ref[:, 2, :, h:h + 1]
        k_h = c_ref[:, 0, h]
        v_h = c_ref[:, 1, h]
        co_ref[:, 0, h] = jnp.where(is_newest, k_col, pltpu.roll(k_h, w - 1, axis=2))
        co_ref[:, 1, h] = jnp.where(is_newest, v_col, pltpu.roll(v_h, w - 1, axis=2))
        s = jnp.sum(k_h * q_col, axis=1, keepdims=True) - slope * dist
        s = jnp.where(valid, s, NEG_BIG)
        s0 = jnp.sum(k_col * q_col, axis=1, keepdims=True)
        m = jnp.maximum(jnp.max(s, axis=-1, keepdims=True), s0)
        p = jnp.exp(s - m)
        p0 = jnp.exp(s0 - m)
        l = jnp.sum(p, axis=-1, keepdims=True) + p0
        outs.append((jnp.sum(v_h * p, axis=-1, keepdims=True) + p0 * v_col) / l)
        lses.append(m + jnp.log(l))
    o_ref[...] = jnp.concatenate(outs, axis=-1)
    l_ref[...] = jnp.concatenate(lses, axis=-1)


def _shift_attend(cache_t, new_t, prev_out, *, layer, group, block_bytes=8 * 1024 * 1024):
    n_att, b = cache_t.shape[:2]
    w = cache_t.shape[-1]
    bb = _pick_tile(b, max(1, block_bytes // (2 * SWA_WIDTH * w * 4)))
    blk = (None, bb, 2, SWA_HEADS, SWA_HEAD_DIM, w)
    in_specs = [pl.BlockSpec(blk, lambda i: (layer, i, 0, 0, 0, 0)),
                pl.BlockSpec((bb, 3, SWA_HEAD_DIM, SWA_HEADS), lambda i: (i, 0, 0, 0))]
    args = [cache_t, new_t]
    aliases = {}
    if prev_out is not None:
        in_specs.append(pl.BlockSpec(memory_space=pl.ANY))
        args.append(prev_out)
        aliases = {2: 0}
    return pl.pallas_call(
        functools.partial(_shift_attend_kernel, group=group, has_prev=prev_out is not None),
        grid=(b // bb,),
        in_specs=in_specs,
        out_specs=[pl.BlockSpec(blk, lambda i: (layer, i, 0, 0, 0, 0)),
                   pl.BlockSpec((bb, SWA_HEAD_DIM, SWA_HEADS), lambda i: (i, 0, 0)),
                   pl.BlockSpec((bb, 1, SWA_HEADS), lambda i: (i, 0, 0))],
        out_shape=[jax.ShapeDtypeStruct(cache_t.shape, cache_t.dtype),
                   jax.ShapeDtypeStruct((b, SWA_HEAD_DIM, SWA_HEADS), F32),
                   jax.ShapeDtypeStruct((b, 1, SWA_HEADS), F32)],
        input_output_aliases=aliases,
        compiler_params=_cparams(("parallel",)),
        name="swa_shift_attend_g%d" % group,
    )(*args)


def _group_merge_kernel(o0, o1, o2, l0, l1, l2, y_ref):
    a0, a1, a2 = l0[...], l1[...], l2[...]
    m = jnp.maximum(jnp.maximum(a0, a1), a2)
    e0, e1, e2 = jnp.exp(a0 - m), jnp.exp(a1 - m), jnp.exp(a2 - m)
    y_ref[...] = (e0 * o0[...] + e1 * o1[...] + e2 * o2[...]) / (e0 + e1 + e2)


def _group_merge(outs, lses):
    return pl.pallas_call(
        _group_merge_kernel,
        out_shape=jax.ShapeDtypeStruct(outs[0].shape, F32),
        name="swa_group_merge",
    )(*outs, *lses)


def _rwkv_proj_kernel(*refs, seq_mode, has_vres, tiles_per_seq):
    it = iter(refs)
    x_ref = next(it)
    prev_ref = next(it)
    vfirst_ref = next(it) if has_vres else None
    g_ref, mix_ref, wr_ref, wk_ref, wv_ref = next(it), next(it), next(it), next(it), next(it)
    w0_ref, w1_ref, w2_ref = next(it), next(it), next(it)
    a0_ref, a1_ref, a2_ref = next(it), next(it), next(it)
    if has_vres:
        v0_ref, v1_ref, v2_ref = next(it), next(it), next(it)
    g1_ref, g2_ref, kk_ref, ka_ref = next(it), next(it), next(it), next(it)
    hsel_ref, hselt_ref = next(it), next(it)
    r_out, ld_out, k_out, v_out, a_out, b_out, g_out, xn_last_out = (next(it) for _ in range(8))

    xn = _rms(x_ref[...], g_ref[...])
    tm = xn.shape[0]
    if seq_mode:
        i = pl.program_id(0)
        pr = _rms(prev_ref[...], g_ref[...])[7:8]
        pr = jnp.where(i % tiles_per_seq == 0, 0.0, pr)
        row = lax.broadcasted_iota(jnp.int32, xn.shape, 0)
        xp = jnp.where(row == 0, pr, pltpu.roll(xn, 1, axis=0))
    else:
        xp = prev_ref[...]
    xx = xp - xn
    mix = mix_ref[...]
    xr, xw, xk, xv, xa, xg = (xn + xx * mix[j:j + 1] for j in range(6))
    r = _bdot(xr, wr_ref[...])
    k = _bdot(xk, wk_ref[...])
    v = _bdot(xv, wv_ref[...])
    z = w0_ref[...] + _bdot(jnp.tanh(_bdot(xw, w1_ref[...])), w2_ref[...])
    w_log = -(jnp.maximum(-z, 0.0) + jnp.log(1.0 + jnp.exp(-jnp.abs(z)))) - 0.5
    ld = -jnp.exp(w_log)
    a = _sigmoid(a0_ref[...] + _bdot(_bdot(xa, a1_ref[...]), a2_ref[...]))
    g = _bdot(_sigmoid(_bdot(xg, g1_ref[...])), g2_ref[...])
    if has_vres:
        v = v + (vfirst_ref[...] - v) * _sigmoid(v0_ref[...] + _bdot(_bdot(xv, v1_ref[...]), v2_ref[...]))
    kk = k * kk_ref[...]
    ssq = jnp.dot((kk * kk).astype(BF16), hsel_ref[...], preferred_element_type=F32)
    rs = lax.rsqrt(jnp.maximum(ssq, 1e-24))
    kk = kk * _dot_sel(rs, hselt_ref[...])
    k = k * (1.0 + (a - 1.0) * ka_ref[...])
    r_out[...] = r
    ld_out[...] = ld
    k_out[...] = k
    v_out[...] = v
    a_out[...] = -kk
    b_out[...] = kk * a
    g_out[...] = g
    xn_last_out[...] = xn[tm - 8:tm] if seq_mode else xn


def _head_selectors(d):
    heads = d // RW_HEAD
    ch = jnp.arange(d) // RW_HEAD
    hsel = (ch[:, None] == jnp.arange(128)[None, :]).astype(BF16)
    return hsel, hsel.T


def _rwkv_proj(x, prev, v_first, p, *, seq_mode, seq, tm_target=256):
    n, d = x.shape
    tm = _pick_tile(seq if seq_mode else n, tm_target)
    has_vres = v_first is not None
    row = lambda a: a.reshape(1, -1)
    full = lambda a: pl.BlockSpec(a.shape, lambda i: (0,) * a.ndim)
    tile = pl.BlockSpec((tm, d), lambda i: (i, 0))
    args, specs = [x], [tile]
    if seq_mode:
        args.append(x)
        specs.append(pl.BlockSpec((8, d), lambda i: (jnp.maximum(i * (tm // 8) - 1, 0), 0)))
    else:
        args.append(prev)
        specs.append(tile)
    if has_vres:
        args.append(v_first)
        specs.append(tile)
    hsel, hselt = _head_selectors(d)
    consts = [row(p["norm_g"]), p["mix"], p["w_r"], p["w_k"], p["w_v"],
              row(p["w0"]), p["w1"], p["w2"], row(p["a0"]), p["a1"], p["a2"]]
    if has_vres:
        consts += [row(p["v0"]), p["v1"], p["v2"]]
    consts += [p["g1"], p["g2"], row(p["k_k"]), row(p["k_a"]), hsel, hselt]
    args += consts
    specs += [full(c) for c in consts]
    outs = pl.pallas_call(
        functools.partial(_rwkv_proj_kernel, seq_mode=seq_mode, has_vres=has_vres,
                          tiles_per_seq=(seq // tm) if seq_mode else 1),
        grid=(n // tm,),
        in_specs=specs,
        out_specs=[tile] * 7 + [pl.BlockSpec((8, d), lambda i: (i, 0)) if seq_mode else tile],
        out_shape=[jax.ShapeDtypeStruct((n, d), F32)] * 7
        + [jax.ShapeDtypeStruct((n // tm * 8 if seq_mode else n, d), F32)],
        compiler_params=_cparams(("parallel",)),
        name="rwkv_proj",
    )(*args)
    return outs


def _wkv_chunk_kernel(r_ref, ld_ref, k_ref, v_ref, a_ref, b_ref, lnw_ref, lnb_ref, rk_ref,
                      y_ref, s_out_ref, s_ref, pre_ref, mat_ref, gam_ref, *, n_chunks):
    C = WKV_CHUNK
    HD = RW_HEAD
    P = 2 * RW_HEAD
    t = pl.program_id(2)
    slot_wr = t % 2
    slot_rd = 1 - slot_wr

    @pl.when(t == 0)
    def _():
        s_ref[...] = jnp.zeros_like(s_ref)
        pre_ref[1] = jnp.zeros(pre_ref.shape[1:], F32)
        mat_ref[1] = jnp.zeros(mat_ref.shape[1:], F32)
        gam_ref[1] = jnp.zeros(gam_ref.shape[1:], F32)

    ri = lax.broadcasted_iota(jnp.int32, (C, C), 0)
    ci = lax.broadcasted_iota(jnp.int32, (C, C), 1)
    tril = ri >= ci
    stril = ri > ci
    tri_ones = tril.astype(BF16)
    first = lambda rows: lax.broadcasted_iota(jnp.int32, (rows, P), 1) < HD
    first_h, first_c, first_2c = first(HD), first(C), first(2 * C)
    block_ones = (lax.broadcasted_iota(jnp.int32, (P, P), 0) // HD
                  == lax.broadcasted_iota(jnp.int32, (P, P), 1) // HD).astype(BF16)
    lnw, lnb, rk = lnw_ref[...], lnb_ref[...], rk_ref[...]

    chunks = []
    chains = []
    for c in range(n_chunks):
        sl = slice(c * C, (c + 1) * C)
        rc, ldc, kc, vc, ac, bc = (z[sl, :] for z in (r_ref, ld_ref, k_ref, v_ref, a_ref, b_ref))
        cum = _sel_dot(tri_ones, ldc)
        e_pos = jnp.exp(cum)
        e_neg = jnp.exp(-cum)
        at = ac * jnp.exp(cum - ldc)
        rt = rc * e_pos
        bt = bc * e_neg
        kt = kc * e_neg
        last = cum[C - 1:C]
        e_last = jnp.exp(last - cum)
        bh = bc * e_last
        kh = kc * e_last
        gamma = jnp.exp(last)
        pre_ref[slot_wr, c, 2] = _bdot(rc * kc * rk, block_ones) * vc
        gam_ref[slot_wr, c] = jnp.broadcast_to(gamma, (8, P))
        chunk = dict(at_swapped=pltpu.roll(at, HD, axis=1), rt=rt, v=vc,
                     ar=jnp.concatenate([at, rt], axis=0), bk=jnp.concatenate([bt, kt], axis=0),
                     bkh=jnp.concatenate([bh, kh], axis=0))
        chunks.append(chunk)
        chains += [dict(h=0, chunk=chunk), dict(h=1, chunk=chunk)]

    def stage_scores():
        for ch in chains:
            ck = ch["chunk"]
            mine = first_2c if ch["h"] == 0 else ~first_2c
            sc = _bdot_nt(jnp.where(mine, ck["ar"], 0.0), ck["bk"])
            ch["lp"] = jnp.where(stril, sc[:C, :C], 0.0)
            ch["ak"] = jnp.where(stril, sc[:C, C:], 0.0)
            ch["rbk"] = jnp.concatenate([jnp.where(tril, sc[C:, :C], 0.0),
                                         jnp.where(tril, sc[C:, C:], 0.0)], axis=1)

    def stage_x0():
        for ch in chains:
            ck = ch["chunk"]
            akv = _bdot(ch["ak"], ck["v"])
            ch["x"] = (jnp.where(first_c, akv, ck["at_swapped"]) if ch["h"] == 0
                       else jnp.where(first_c, ck["at_swapped"], akv))

    def stage_level(square):
        nxt = [ch["x"] + _bdot(ch["lp"], ch["x"]) for ch in chains]
        if square:
            for ch in chains:
                ch["lp"] = _bdot(ch["lp"], ch["lp"])
        for ch, x in zip(chains, nxt):
            ch["x"] = x

    def stage_final_dots():
        for ch in chains:
            ck = ch["chunk"]
            vz = jnp.where(first_c, ck["v"], 0.0) if ch["h"] == 0 else jnp.where(first_c, 0.0, ck["v"])
            z = jnp.concatenate([ch["x"], vz], axis=0)
            ch["ry"] = _bdot(ch["rbk"], z)
            ch["qh"] = _bdot_tn(z, ck["bkh"])

    def stage_final_stores():
        for c, ck in enumerate(chunks):
            ry_a, ry_b = chains[2 * c]["ry"], chains[2 * c + 1]["ry"]
            qa, qb = chains[2 * c]["qh"], chains[2 * c + 1]["qh"]
            pre_ref[slot_wr, c, 0] = ck["rt"] + pltpu.roll(jnp.where(first_c, ry_b, ry_a), HD, axis=1)
            pre_ref[slot_wr, c, 1] = jnp.where(first_c, ry_a, ry_b)
            mat_ref[slot_wr, c, 0] = jnp.concatenate([jnp.where(first_h, qa[HD:], 0.0),
                                                      jnp.where(first_h, 0.0, qb[:HD])], axis=0)
            mat_ref[slot_wr, c, 1] = jnp.concatenate([jnp.where(first_h, qa[:HD], 0.0),
                                                      jnp.where(first_h, 0.0, qb[HD:])], axis=0)

    state = [s_ref[...]]

    def state_step(c):
        s0 = state[0]
        y = _bdot_nt(pre_ref[slot_rd, c, 0], s0) + pre_ref[slot_rd, c, 1]
        state[0] = s0 * gam_ref[slot_rd, c, 0:1] + _bdot(s0, mat_ref[slot_rd, c, 0]) + mat_ref[slot_rd, c, 1]
        mu = _bdot(y, block_ones) * (1.0 / HD)
        yc = y - mu
        var = _bdot(yc * yc, block_ones) * (1.0 / HD)
        y_ref[c * C:(c + 1) * C, :] = yc * lax.rsqrt(var + RW_LN_EPS) * lnw + lnb + pre_ref[slot_rd, c, 2]

    stages = [stage_scores, stage_x0]
    n = 1
    while n < C:
        stages.append(functools.partial(stage_level, 2 * n < C))
        n *= 2
    stages += [stage_final_dots, stage_final_stores]
    for si, stage in enumerate(stages):
        stage()
        for c in range(si * n_chunks // len(stages), (si + 1) * n_chunks // len(stages)):
            state_step(c)
    s_ref[...] = state[0]

    @pl.when(t == pl.num_programs(2) - 1)
    def _():
        s_out_ref[0] = s_ref[0:HD, 0:HD]
        s_out_ref[1] = s_ref[HD:P, HD:P]


def _wkv_chunked(r, ld, k, v, a, b, ln_w, ln_b, r_k, *, bsz, seq, tb_target=512):
    n, d = r.shape
    heads = d // RW_HEAD
    pairs = heads // 2
    tb = _pick_tile(seq, tb_target)
    per_seq = seq // tb
    n_chunks = tb // WKV_CHUNK
    in_tile = pl.BlockSpec((tb, 128), lambda bi, hp, t: (bi * per_seq + jnp.minimum(t, per_seq - 1), hp))
    out_tile = pl.BlockSpec((tb, 128), lambda bi, hp, t: (bi * per_seq + jnp.maximum(t - 1, 0), hp))
    vec = pl.BlockSpec((1, 128), lambda bi, hp, t: (0, hp))
    y, st = pl.pallas_call(
        functools.partial(_wkv_chunk_kernel, n_chunks=n_chunks),
        grid=(bsz, pairs, per_seq + 1),
        in_specs=[in_tile] * 6 + [vec] * 3,
        out_specs=[out_tile, pl.BlockSpec((None, 2, RW_HEAD, RW_HEAD), lambda bi, hp, t: (bi, hp, 0, 0))],
        out_shape=[jax.ShapeDtypeStruct((n, d), F32),
                   jax.ShapeDtypeStruct((bsz, heads, RW_HEAD, RW_HEAD), F32)],
        scratch_shapes=[pltpu.VMEM((128, 128), F32),
                        pltpu.VMEM((2, n_chunks, 3, WKV_CHUNK, 128), F32),
                        pltpu.VMEM((2, n_chunks, 2, 128, 128), F32),
                        pltpu.VMEM((2, n_chunks, 8, 128), F32)],
        compiler_params=_cparams(("parallel", "parallel", "arbitrary")),
        name="wkv_chunked",
    )(r, ld, k, v, a, b, ln_w.reshape(1, d), ln_b.reshape(1, d), r_k.reshape(1, d))
    return y, st


def _wkv_step_kernel(*refs, has_prev):
    it = iter(refs)
    s_ref, r_ref, ld_ref, k_ref, a_ref, b_ref, rk_ref, v_ref, lnw_ref, lnb_ref = (next(it) for _ in range(10))
    if has_prev:
        next(it)
    s_out, y_out = next(it), next(it)
    s = s_ref[...]
    r, k = r_ref[...], k_ref[...]
    v = v_ref[...]
    sa = jnp.sum(s * a_ref[...][:, None], axis=2, keepdims=True)
    s_new = s * jnp.exp(ld_ref[...])[:, None] + sa * b_ref[...][:, None] + v * k[:, None]
    y = jnp.sum(s_new * r[:, None], axis=2, keepdims=True)
    mu = jnp.mean(y, axis=1, keepdims=True)
    yc = y - mu
    var = jnp.mean(yc * yc, axis=1, keepdims=True)
    yn = yc * lax.rsqrt(var + RW_LN_EPS) * lnw_ref[...] + lnb_ref[...]
    bonus = jnp.sum(r * k * rk_ref[...], axis=1, keepdims=True)[:, None] * v
    s_out[...] = s_new
    y_out[...] = yn + bonus


def _wkv_step(state_t, prev_out, layer, r, ld, k, v, a, b, ln_w, ln_b, r_k, *, hb=2):
    heads, bsz = state_t.shape[1], state_t.shape[-1]
    d = heads * RW_HEAD
    per_key = lambda z: z.T.reshape(heads, RW_HEAD, bsz)
    per_val = lambda z: z.T.reshape(heads, RW_HEAD, 1, bsz)
    bcast = lambda p, shape: jnp.broadcast_to(p.reshape(shape[:-1] + (1,)), shape)
    st_spec = pl.BlockSpec((None, hb, RW_HEAD, RW_HEAD, bsz), lambda i: (layer, i, 0, 0, 0))
    key_spec = pl.BlockSpec((hb, RW_HEAD, bsz), lambda i: (i, 0, 0))
    val_spec = pl.BlockSpec((hb, RW_HEAD, 1, bsz), lambda i: (i, 0, 0, 0))
    in_specs = [st_spec] + [key_spec] * 6 + [val_spec] * 3
    args = [state_t, per_key(r), per_key(ld), per_key(k), per_key(a), per_key(b),
            bcast(r_k, (heads, RW_HEAD, bsz)), per_val(v),
            bcast(ln_w, (heads, RW_HEAD, 1, bsz)), bcast(ln_b, (heads, RW_HEAD, 1, bsz))]
    aliases = {}
    if prev_out is not None:
        in_specs.append(pl.BlockSpec(memory_space=pl.ANY))
        args.append(prev_out)
        aliases = {10: 0}
    s_new, y = pl.pallas_call(
        functools.partial(_wkv_step_kernel, has_prev=prev_out is not None),
        grid=(heads // hb,),
        in_specs=in_specs,
        out_specs=[st_spec, val_spec],
        out_shape=[jax.ShapeDtypeStruct(state_t.shape, F32),
                   jax.ShapeDtypeStruct((heads, RW_HEAD, 1, bsz), F32)],
        input_output_aliases=aliases,
        compiler_params=_cparams(("parallel",)),
        name="wkv_step",
    )(*args)
    return y.reshape(d, bsz).T, s_new


def _run_layers(x, mem_kv, wkv0, shift0, swa_bufs, W, *, bsz, seq):
    is_prompt = swa_bufs is None
    depth = W["norm_g"].shape[0]
    new_wkv, new_shift = [], []
    new_kv = [[] for _ in range(N_GROUPS)]
    v_first = None
    for l in range(depth):
        ng = W["norm_g"][l]
        x = _ffn(x, ng[0], W["ffn_w_in"][l, 0], W["ffn_w_out"][l, 0])
        i = l // 2
        if l % 2 == 0:
            p = dict(norm_g=ng[1], mix=W["rw_mix"][i], w_r=W["rw_w_rkv"][i, 0], w_k=W["rw_w_rkv"][i, 1],
                     w_v=W["rw_w_rkv"][i, 2], w0=W["rw_w0"][i], w1=W["rw_w1"][i], w2=W["rw_w2"][i],
                     a0=W["rw_a0"][i], a1=W["rw_a1"][i], a2=W["rw_a2"][i], g1=W["rw_g1"][i], g2=W["rw_g2"][i],
                     k_k=W["rw_k_k"][i], k_a=W["rw_k_a"][i])
            if i > 0:
                p.update(v0=W["rw_v0"][i - 1], v1=W["rw_v1"][i - 1], v2=W["rw_v2"][i - 1])
            prev = None if is_prompt else shift0[i]
            r, ld, k, v, a, b, g, xn_last = _rwkv_proj(x, prev, v_first if i > 0 else None, p,
                                                      seq_mode=is_prompt, seq=seq)
            if i == 0:
                v_first = v
            if is_prompt:
                y, s_new = _wkv_chunked(r, ld, k, v, a, b, W["rw_ln_w"][i], W["rw_ln_b"][i],
                                        W["rw_r_k"][i].reshape(-1), bsz=bsz, seq=seq)
                tiles = xn_last.shape[0] // 8 // bsz
                shift_new = xn_last.reshape(bsz, tiles, 8, -1)[:, -1, -1]
                new_wkv.append(s_new)
            else:
                y, s_new = _wkv_step(wkv0, new_wkv[0] if new_wkv else None, i, r, ld, k, v, a, b,
                                     W["rw_ln_w"][i], W["rw_ln_b"][i], W["rw_r_k"][i].reshape(-1))
                shift_new = xn_last
                new_wkv = [s_new]
            new_shift.append(shift_new)
            x = _linear(y, W["rw_w_o"][i], mul=g, res=x, name="rwkv_out")
        else:
            outs, lses = [], []
            if is_prompt:
                qkv_groups = _swa_qkv_prompt(x, ng[1], W["swa_w_qkv"][i], bsz=bsz, seq=seq)
                for gi in range(N_GROUPS):
                    o, lse = _band_attention(qkv_groups[gi], gi)
                    outs.append(o)
                    lses.append(lse)
                    dil = SWA_DILATIONS[gi]
                    keep = min(SWA_WINDOWS[gi], seq)
                    kv_tail = qkv_groups[gi][:, :, (seq - keep) // dil:, SWA_WIDTH:]
                    new_kv[gi].append(jnp.swapaxes(kv_tail, 1, 2).reshape(
                        bsz, keep, 2, SWA_HEADS, SWA_HEAD_DIM))
                x = _swa_merge(outs, lses, x, W["swa_w_o"][i], seq=seq)
            else:
                qkv = _linear(x, W["swa_w_qkv"][i], norm_g=ng[1], tn_target=512, name="swa_qkv")
                qkv5 = qkv.reshape(bsz, N_GROUPS, 3, SWA_HEADS, SWA_HEAD_DIM)
                for gi in range(N_GROUPS):
                    prev_out = new_kv[gi][0] if new_kv[gi] else None
                    new_t = jnp.swapaxes(qkv5[:, gi], -1, -2)
                    cache_out, o, lse = _shift_attend(swa_bufs[gi], new_t, prev_out, layer=i, group=gi)
                    new_kv[gi] = [cache_out]
                    outs.append(o)
                    lses.append(lse)
                y = jnp.swapaxes(_group_merge(outs, lses), 1, 2).reshape(bsz, SWA_WIDTH)
                x = _linear(y, W["swa_w_o"][i], res=x, name="swa_out")
        if is_prompt:
            x = _xa_prompt(x, ng[2], W["xa_w_q"][l], mem_kv[l], W["xa_w_o"][l], seq=seq)
        else:
            q = _linear(x, W["xa_w_q"][l], norm_g=ng[2], name="xa_q")
            o = _xa_sample(q, mem_kv, l)
            x = _linear(o, W["xa_w_o"][l], res=x, name="xa_out")
        x = _ffn(x, ng[3], W["ffn_w_in"][l, 1], W["ffn_w_out"][l, 1],
                 final_g=W["final_norm_g"] if l == depth - 1 else None)
    return x, new_wkv, new_shift, new_kv


def kernel(x_prompt, x_sample, cache_mem_kv, state_rwkv_wkv, state_rwkv_shift, cache_swa_kv_g0, cache_swa_kv_g1, cache_swa_kv_g2, mem_prompt, norm_g, mem_norm_g, final_norm_g, ffn_w_in, ffn_w_out, rw_mix, rw_w_rkv, rw_w_o, rw_w0, rw_w1, rw_w2, rw_a0, rw_a1, rw_a2, rw_v0, rw_v1, rw_v2, rw_g1, rw_g2, rw_k_k, rw_k_a, rw_r_k, rw_ln_w, rw_ln_b, swa_w_qkv, swa_w_o, xa_w_q, xa_w_kv, xa_w_o):
    bf = lambda w: w.astype(BF16)
    W = dict(norm_g=norm_g, final_norm_g=final_norm_g, ffn_w_in=bf(ffn_w_in), ffn_w_out=bf(ffn_w_out),
             rw_mix=rw_mix, rw_w_rkv=bf(rw_w_rkv), rw_w_o=bf(rw_w_o), rw_w0=rw_w0, rw_w1=bf(rw_w1),
             rw_w2=bf(rw_w2), rw_a0=rw_a0, rw_a1=bf(rw_a1), rw_a2=bf(rw_a2), rw_v0=rw_v0, rw_v1=bf(rw_v1),
             rw_v2=bf(rw_v2), rw_g1=bf(rw_g1), rw_g2=bf(rw_g2), rw_k_k=rw_k_k, rw_k_a=rw_k_a, rw_r_k=rw_r_k,
             rw_ln_w=rw_ln_w, rw_ln_b=rw_ln_b, swa_w_qkv=bf(swa_w_qkv), swa_w_o=bf(swa_w_o),
             xa_w_q=bf(xa_w_q), xa_w_o=bf(xa_w_o))
    depth = norm_g.shape[0]
    bsz, seq, d = x_prompt.shape
    n_mem = mem_prompt.shape[1]

    mem2 = mem_prompt.reshape(bsz * n_mem, d)
    xa_w_kv_b = bf(xa_w_kv)
    mem_kv_p = jnp.stack([_linear(mem2, xa_w_kv_b[l], norm_g=mem_norm_g[l], name="mem_kv")
                          for l in range(depth)])
    y_p, wkv_p, shift_p, kv_p = _run_layers(
        x_prompt.reshape(bsz * seq, d), mem_kv_p.reshape(depth, bsz, n_mem, 2 * XA_WIDTH),
        None, None, None, W, bsz=bsz, seq=seq)

    dbsz, dseq, _ = x_sample.shape
    assert dseq == 1
    caches = tuple(jnp.transpose(c, (0, 1, 3, 4, 5, 2))
                   for c in (cache_swa_kv_g0, cache_swa_kv_g1, cache_swa_kv_g2))
    wkv_t = jnp.transpose(state_rwkv_wkv, (0, 2, 3, 4, 1))
    y_s, wkv_s, shift_s, kv_s = _run_layers(
        x_sample.reshape(dbsz, d), cache_mem_kv,
        wkv_t, state_rwkv_shift, caches, W, bsz=dbsz, seq=1)
    swa_s = tuple(jnp.transpose(kv_s[g][0], (0, 1, 5, 2, 3, 4)) for g in range(N_GROUPS))
    wkv_s = jnp.transpose(wkv_s[0], (0, 4, 1, 2, 3))

    return (y_p.reshape(bsz, seq, d), y_s.reshape(dbsz, 1, d),
            mem_kv_p.reshape(depth, bsz, n_mem, 2, XA_HEADS, XA_HEAD_DIM),
            jnp.stack(wkv_p), jnp.stack(shift_p),
            jnp.stack(kv_p[0]), jnp.stack(kv_p[1]), jnp.stack(kv_p[2]),
            wkv_s, jnp.stack(shift_s), swa_s[0], swa_s[1], swa_s[2])
```

```python
import functools

import jax
import jax.numpy as jnp
from jax import lax
from jax.experimental import pallas as pl
from jax.experimental.pallas import tpu as pltpu

F32 = jnp.float32
BF16 = jnp.bfloat16

NORM_EPS = 1e-6
RW_HEAD = 64
RW_LN_EPS = RW_HEAD * 1e-5
SWA_WINDOWS = (128, 512, 2048)
SWA_DILATIONS = (1, 4, 16)
N_GROUPS = 3
SWA_HEADS = 8
SWA_HEAD_DIM = 64
SWA_WIDTH = SWA_HEADS * SWA_HEAD_DIM
SWA_BLOCK = 128
XA_HEADS = 4
XA_HEAD_DIM = 128
XA_WIDTH = XA_HEADS * XA_HEAD_DIM
WKV_CHUNK = 64
NEG_BIG = -1e30
VMEM_LIMIT = 56 * 1024 * 1024


def _cparams(sem):
    return pltpu.CompilerParams(dimension_semantics=sem, vmem_limit_bytes=VMEM_LIMIT)


def _bdot(a, b):
    return jnp.dot(a.astype(BF16), b.astype(BF16), preferred_element_type=F32)


def _bdot_nt(a, b):
    return lax.dot_general(a.astype(BF16), b.astype(BF16), (((1,), (1,)), ((), ())),
                           preferred_element_type=F32)


def _bdot_tn(a, b):
    return lax.dot_general(a.astype(BF16), b.astype(BF16), (((0,), (0,)), ((), ())),
                           preferred_element_type=F32)


def _split3(x):
    h1 = x.astype(BF16)
    r1 = x - h1.astype(F32)
    h2 = r1.astype(BF16)
    h3 = (r1 - h2.astype(F32)).astype(BF16)
    return h1, h2, h3


def _sel_dot(sel_bf16, x):
    h1, h2, h3 = _split3(x)
    d = lambda h: jnp.dot(sel_bf16, h, preferred_element_type=F32)
    return d(h1) + d(h2) + d(h3)


def _dot_sel(x, sel_bf16):
    h1, h2, h3 = _split3(x)
    d = lambda h: jnp.dot(h, sel_bf16, preferred_element_type=F32)
    return d(h1) + d(h2) + d(h3)


def _rms(x, g):
    return x * lax.rsqrt(jnp.mean(x * x, axis=-1, keepdims=True) + NORM_EPS) * g


def _sigmoid(x):
    return 1.0 / (1.0 + jnp.exp(-x))


def _pick_tile(n, target):
    t = min(n, target)
    while n % t:
        t //= 2
    return t


def _ffn_kernel(*refs, final_norm):
    if final_norm:
        x_ref, g_ref, wg_ref, wu_ref, wo_ref, fg_ref, o_ref, xn_ref, acc_ref = refs
    else:
        x_ref, g_ref, wg_ref, wu_ref, wo_ref, o_ref, xn_ref, acc_ref = refs
    j = pl.program_id(1)

    @pl.when(j == 0)
    def _():
        xn_ref[...] = _rms(x_ref[...], g_ref[...]).astype(BF16)
        acc_ref[...] = jnp.zeros_like(acc_ref)

    xn = xn_ref[...]
    gate = jnp.dot(xn, wg_ref[...], preferred_element_type=F32)
    up = jnp.dot(xn, wu_ref[...], preferred_element_type=F32)
    h = (gate * _sigmoid(gate) * up).astype(BF16)
    acc_ref[...] += jnp.dot(h, wo_ref[...], preferred_element_type=F32)

    @pl.when(j == pl.num_programs(1) - 1)
    def _():
        y = x_ref[...] + 0.5 * acc_ref[...]
        o_ref[...] = _rms(y, fg_ref[...]) if final_norm else y


def _ffn(x, g, w_in, w_out, layer, which, *, final_g=None, tm_target=512, tf_target=1408):
    n, d = x.shape
    d_ff = w_out.shape[2]
    tm = _pick_tile(n, tm_target)
    tf = tf_target if d_ff % tf_target == 0 else 128
    nf = d_ff // tf
    vec = pl.BlockSpec((1, d), lambda i, j: (0, 0))
    in_specs = [
        pl.BlockSpec((tm, d), lambda i, j: (i, 0)),
        vec,
        pl.BlockSpec((None, None, d, tf), lambda i, j: (layer, which, 0, j)),
        pl.BlockSpec((None, None, d, tf), lambda i, j: (layer, which, 0, j + nf)),
        pl.BlockSpec((None, None, tf, d), lambda i, j: (layer, which, j, 0)),
    ]
    args = [x, g.reshape(1, d), w_in, w_in, w_out]
    if final_g is not None:
        in_specs.append(vec)
        args.append(final_g.reshape(1, d))
    return pl.pallas_call(
        functools.partial(_ffn_kernel, final_norm=final_g is not None),
        grid=(n // tm, nf),
        in_specs=in_specs,
        out_specs=pl.BlockSpec((tm, d), lambda i, j: (i, 0)),
        out_shape=jax.ShapeDtypeStruct((n, d), F32),
        scratch_shapes=[pltpu.VMEM((tm, d), BF16), pltpu.VMEM((tm, d), F32)],
        compiler_params=_cparams(("parallel", "arbitrary")),
        name="ffn",
    )(*args)


def _linear_kernel(*refs, mode, has_res):
    it = iter(refs)
    x_ref = next(it)
    p_ref = next(it) if mode in ("norm", "mul") else None
    w_ref = next(it)
    res_ref = next(it) if has_res else None
    o_ref = next(it)
    xs_ref = next(it)

    @pl.when(pl.program_id(1) == 0)
    def _():
        x = x_ref[...]
        if mode == "norm":
            x = _rms(x, p_ref[...])
        elif mode == "mul":
            x = x * p_ref[...]
        xs_ref[...] = x.astype(BF16)

    o = jnp.dot(xs_ref[...], w_ref[...], preferred_element_type=F32)
    if has_res:
        o = res_ref[...] + o
    o_ref[...] = o


def _linear(x, w, *, norm_g=None, mul=None, res=None, tm_target=512, tn_target=1024, name="linear"):
    n, kd = x.shape
    nout = w.shape[1]
    tm = _pick_tile(n, tm_target)
    tn = _pick_tile(nout, tn_target)
    mode = "norm" if norm_g is not None else ("mul" if mul is not None else "plain")
    args = [x]
    specs = [pl.BlockSpec((tm, kd), lambda i, j: (i, 0))]
    if mode == "norm":
        args.append(norm_g.reshape(1, kd))
        specs.append(pl.BlockSpec((1, kd), lambda i, j: (0, 0)))
    elif mode == "mul":
        args.append(mul)
        specs.append(pl.BlockSpec((tm, kd), lambda i, j: (i, 0)))
    args.append(w)
    specs.append(pl.BlockSpec((kd, tn), lambda i, j: (0, j)))
    if res is not None:
        args.append(res)
        specs.append(pl.BlockSpec((tm, tn), lambda i, j: (i, j)))
    return pl.pallas_call(
        functools.partial(_linear_kernel, mode=mode, has_res=res is not None),
        grid=(n // tm, nout // tn),
        in_specs=specs,
        out_specs=pl.BlockSpec((tm, tn), lambda i, j: (i, j)),
        out_shape=jax.ShapeDtypeStruct((n, nout), F32),
        scratch_shapes=[pltpu.VMEM((tm, kd), BF16)],
        compiler_params=_cparams(("parallel", "arbitrary")),
        name=name,
    )(*args)


def _xa_prompt_kernel(x_ref, g_ref, wq_ref, kv_ref, wo_ref, o_ref):
    x = x_ref[...]
    q = jnp.dot(_rms(x, g_ref[...]).astype(BF16), wq_ref[...], preferred_element_type=F32)
    kv = kv_ref[...].astype(BF16)
    outs = []
    for h in range(XA_HEADS):
        lo = h * XA_HEAD_DIM
        s = _bdot_nt(q[:, lo:lo + XA_HEAD_DIM], kv[:, lo:lo + XA_HEAD_DIM]) * (XA_HEAD_DIM ** -0.5)
        m = jnp.max(s, axis=-1, keepdims=True)
        p = jnp.exp(s - m)
        l = jnp.sum(p, axis=-1, keepdims=True)
        o = jnp.dot(p.astype(BF16), kv[:, XA_WIDTH + lo:XA_WIDTH + lo + XA_HEAD_DIM],
                    preferred_element_type=F32)
        outs.append(o / l)
    o = jnp.concatenate(outs, axis=-1)
    o_ref[...] = x + jnp.dot(o.astype(BF16), wo_ref[...], preferred_element_type=F32)


def _xa_prompt(x, g, wq, kv, wo, *, seq, tm_target=512):
    n, d = x.shape
    n_mem = kv.shape[1]
    tm = _pick_tile(seq, tm_target)
    per_b = seq // tm
    return pl.pallas_call(
        _xa_prompt_kernel,
        grid=(n // tm,),
        in_specs=[
            pl.BlockSpec((tm, d), lambda i: (i, 0)),
            pl.BlockSpec((1, d), lambda i: (0, 0)),
            pl.BlockSpec((d, XA_WIDTH), lambda i: (0, 0)),
            pl.BlockSpec((None, n_mem, 2 * XA_WIDTH), lambda i: (i // per_b, 0, 0)),
            pl.BlockSpec((XA_WIDTH, d), lambda i: (0, 0)),
        ],
        out_specs=pl.BlockSpec((tm, d), lambda i: (i, 0)),
        out_shape=jax.ShapeDtypeStruct((n, d), F32),
        compiler_params=_cparams(("parallel",)),
        name="xa_prompt",
    )(x, g.reshape(1, d), wq, kv, wo)


def _xa_sample_kernel(q_ref, kv_ref, o_ref):
    q = q_ref[...]
    k = kv_ref[:, :, 0]
    v = kv_ref[:, :, 1]
    s = jnp.sum(k * q[:, None], axis=-1, keepdims=True) * (XA_HEAD_DIM ** -0.5)
    m = jnp.max(s, axis=1, keepdims=True)
    p = jnp.exp(s - m)
    l = jnp.sum(p, axis=1)
    o_ref[...] = jnp.sum(p * v, axis=1) / l


def _xa_sample(q, kv, layer, *, bb=8):
    b = q.shape[0]
    n_mem = kv.shape[2]
    bb = _pick_tile(b, bb)
    hd = (XA_HEADS, XA_HEAD_DIM)
    out = pl.pallas_call(
        _xa_sample_kernel,
        grid=(b // bb,),
        in_specs=[
            pl.BlockSpec((bb,) + hd, lambda i: (i, 0, 0)),
            pl.BlockSpec((None, bb, n_mem, 2) + hd, lambda i: (layer, i, 0, 0, 0, 0)),
        ],
        out_specs=pl.BlockSpec((bb,) + hd, lambda i: (i, 0, 0)),
        out_shape=jax.ShapeDtypeStruct((b,) + hd, F32),
        compiler_params=_cparams(("parallel",)),
        name="xa_sample",
    )(q.reshape((b,) + hd), kv)
    return out.reshape(b, XA_WIDTH)


def _band_bias(group):
    dil = SWA_DILATIONS[group]
    units = SWA_WINDOWS[group] // dil
    qi = jnp.arange(SWA_BLOCK)[:, None]
    ki = jnp.arange(2 * SWA_BLOCK)[None, :]
    delta = qi + SWA_BLOCK - ki
    band = (delta >= 0) & (delta <= units)
    heads = jnp.arange(SWA_HEADS, dtype=F32)
    slopes = jnp.exp2(-8.0 * (group * SWA_HEADS + heads + 1.0) / (N_GROUPS * SWA_HEADS))
    alibi = -slopes[:, None, None] * (dil * delta).astype(F32)[None]
    later = jnp.where(band[None], alibi, NEG_BIG)
    first = jnp.where((band & (ki >= SWA_BLOCK))[None], alibi, NEG_BIG)
    return jnp.stack([first, later])


def _band_kernel(q_ref, kp_ref, kc_ref, vp_ref, vc_ref, bias_ref, o_ref, l_ref):
    n = pl.program_id(2)
    blocks = q_ref.shape[0] // SWA_BLOCK
    q = (q_ref[...] * (SWA_HEAD_DIM ** -0.5)).astype(BF16)
    k = jnp.concatenate([kp_ref[...], kc_ref[...]], axis=0).astype(BF16)
    v = jnp.concatenate([vp_ref[...], vc_ref[...]], axis=0).astype(BF16)
    first_of_seq = jnp.minimum(n, 1)
    work = [(sb, h) for sb in range(blocks) for h in range(SWA_HEADS)]
    rows = lambda sb: slice(sb * SWA_BLOCK, (sb + 1) * SWA_BLOCK)
    keys = lambda sb: slice(sb * SWA_BLOCK, (sb + 2) * SWA_BLOCK)
    lanes = lambda h: slice(h * SWA_HEAD_DIM, (h + 1) * SWA_HEAD_DIM)
    ss = [_bdot_nt(q[rows(sb), lanes(h)], k[keys(sb), lanes(h)])
          + (bias_ref[first_of_seq, h] if sb == 0 else bias_ref[1, h]) for sb, h in work]
    ms = [jnp.max(s, axis=-1, keepdims=True) for s in ss]
    ps = [jnp.exp(s - m) for s, m in zip(ss, ms)]
    ls = [jnp.sum(p, axis=-1, keepdims=True) for p in ps]
    os = [jnp.dot(p.astype(BF16), v[keys(sb), lanes(h)], preferred_element_type=F32)
          for p, (sb, h) in zip(ps, work)]
    for (sb, h), o, m, l in zip(work, os, ms, ls):
        o_ref[rows(sb), lanes(h)] = o / l
        l_ref[rows(sb), lanes(h)] = jnp.broadcast_to(m + jnp.log(l), (SWA_BLOCK, SWA_HEAD_DIM))


def _swa_qkv_kernel(x_ref, g_ref, w_ref, o0_ref, o1_ref, o2_ref, tmp_ref):
    xs = _rms(x_ref[...], g_ref[...]).astype(BF16)
    tm = xs.shape[0]
    gw = 3 * SWA_WIDTH
    for g, o_ref in enumerate((o0_ref, o1_ref, o2_ref)):
        dil = SWA_DILATIONS[g]
        out = jnp.dot(xs, w_ref[:, g * gw:(g + 1) * gw], preferred_element_type=F32)
        if dil == 1:
            o_ref[0] = out
        else:
            for cb in range(gw // 128):
                tmp_ref[cb] = out[:, cb * 128:(cb + 1) * 128]
            for r in range(dil):
                for cb in range(gw // 128):
                    o_ref[r, :, cb * 128:(cb + 1) * 128] = tmp_ref[cb, pl.ds(r, tm // dil, stride=dil), :]


def _swa_qkv_prompt(x, g, w, layer, *, bsz, seq, tm_target=512):
    n, d = x.shape
    tm = _pick_tile(seq, tm_target)
    per_b = seq // tm
    gw = 3 * SWA_WIDTH
    out_specs, out_shapes = [], []
    for dil in SWA_DILATIONS:
        out_specs.append(pl.BlockSpec((None, dil, tm // dil, gw), lambda i: (i // per_b, 0, i % per_b, 0)))
        out_shapes.append(jax.ShapeDtypeStruct((bsz, dil, seq // dil, gw), F32))
    return pl.pallas_call(
        _swa_qkv_kernel,
        grid=(n // tm,),
        in_specs=[pl.BlockSpec((tm, d), lambda i: (i, 0)),
                  pl.BlockSpec((1, d), lambda i: (0, 0)),
                  pl.BlockSpec((None,) + w.shape[1:], lambda i: (layer, 0, 0),
                               pipeline_mode=pl.Buffered(1))],
        out_specs=out_specs,
        out_shape=out_shapes,
        scratch_shapes=[pltpu.VMEM((gw // 128, tm, 128), F32)],
        compiler_params=_cparams(("parallel",)),
        name="swa_qkv_prompt",
    )(x, g.reshape(1, d), w)


def _band_attention(qkv_g, group):
    bsz, dil, l_sub, _ = qkv_g.shape
    per_step = 2 if l_sub % (2 * SWA_BLOCK) == 0 else 1
    rows = per_step * SWA_BLOCK
    blk = (None, None, rows, SWA_WIDTH)
    cur = lambda col: pl.BlockSpec(blk, lambda b, r, n: (b, r, n, col))
    prev = lambda col: pl.BlockSpec((None, None, SWA_BLOCK, SWA_WIDTH),
                                    lambda b, r, n: (b, r, jnp.maximum(n * per_step - 1, 0), col))
    out_spec = pl.BlockSpec(blk, lambda b, r, n: (b, r, n, 0))
    bias_spec = pl.BlockSpec((2, SWA_HEADS, SWA_BLOCK, 2 * SWA_BLOCK), lambda b, r, n: (0, 0, 0, 0))
    return pl.pallas_call(
        _band_kernel,
        grid=(bsz, dil, l_sub // rows),
        in_specs=[cur(0), prev(1), cur(1), prev(2), cur(2), bias_spec],
        out_specs=[out_spec, out_spec],
        out_shape=[jax.ShapeDtypeStruct((bsz, dil, l_sub, SWA_WIDTH), F32)] * 2,
        compiler_params=_cparams(("parallel", "parallel", "arbitrary")),
        name="swa_band_g%d" % group,
    )(qkv_g, qkv_g, qkv_g, qkv_g, qkv_g, _band_bias(group))


def _merge_kernel(o0, o1, o2, l0, l1, l2, x_ref, w_ref, out_ref, *tmp_refs):
    tm = x_ref.shape[0]
    vals = [o0[0], l0[0]]
    tmp = iter(tmp_refs)
    for g, (o_ref, l_ref) in ((1, (o1, l1)), (2, (o2, l2))):
        dil = SWA_DILATIONS[g]
        for ref in (o_ref, l_ref):
            t_ref = next(tmp)
            for r in range(dil):
                for cb in range(SWA_WIDTH // 128):
                    t_ref[cb, pl.ds(r, tm // dil, stride=dil), :] = ref[r, :, cb * 128:(cb + 1) * 128]
            vals.append(jnp.concatenate([t_ref[cb] for cb in range(SWA_WIDTH // 128)], axis=-1))
    o_0, a0, o_1, a1, o_2, a2 = vals
    m = jnp.maximum(jnp.maximum(a0, a1), a2)
    e0, e1, e2 = jnp.exp(a0 - m), jnp.exp(a1 - m), jnp.exp(a2 - m)
    y = (e0 * o_0 + e1 * o_1 + e2 * o_2) / (e0 + e1 + e2)
    out_ref[...] = x_ref[...] + jnp.dot(y.astype(BF16), w_ref[...], preferred_element_type=F32)


def _swa_merge(outs, lses, x, wo, *, seq, tm_target=256):
    n, d = x.shape
    tm = _pick_tile(seq, tm_target)
    per_b = seq // tm
    specs = [pl.BlockSpec((None, dil, tm // dil, SWA_WIDTH), lambda i: (i // per_b, 0, i % per_b, 0))
             for dil in SWA_DILATIONS]
    return pl.pallas_call(
        _merge_kernel,
        grid=(n // tm,),
        in_specs=specs + specs + [pl.BlockSpec((tm, d), lambda i: (i, 0)),
                                  pl.BlockSpec((SWA_WIDTH, d), lambda i: (0, 0))],
        out_specs=pl.BlockSpec((tm, d), lambda i: (i, 0)),
        out_shape=jax.ShapeDtypeStruct((n, d), F32),
        scratch_shapes=[pltpu.VMEM((SWA_WIDTH // 128, tm, 128), F32)] * 4,
        compiler_params=_cparams(("parallel",)),
        name="swa_merge",
    )(*outs, *lses, x, wo)


def _shift_attend_kernel(*refs, group, has_prev):
    it = iter(refs)
    c_ref, new_ref = next(it), next(it)
    if has_prev:
        next(it)
    co_ref, o_ref, l_ref = next(it), next(it), next(it)
    dil = SWA_DILATIONS[group]
    w = c_ref.shape[-1]
    pos = lax.broadcasted_iota(jnp.int32, (1, 1, w), 2)
    valid = (pos % dil) == 0
    dist = (w - pos).astype(F32)
    is_newest = pos == w - 1
    heads = range(SWA_HEADS)
    slopes = [2.0 ** (-8.0 * (group * SWA_HEADS + h + 1) / (N_GROUPS * SWA_HEADS)) for h in heads]
    q_cols = [new_ref[:, 0, :, h:h + 1] * (SWA_HEAD_DIM ** -0.5) for h in heads]
    k_cols = [new_ref[:, 1, :, h:h + 1] for h in heads]
    v_cols = [new_ref[:, 2, :, h:h + 1] for h in heads]
    ss, s0s = [], []
    for h in heads:
        k_h = c_ref[:, 0, h]
        co_ref[:, 0, h] = jnp.where(is_newest, k_cols[h], pltpu.roll(k_h, w - 1, axis=2))
        s = jnp.sum(k_h * q_cols[h], axis=1, keepdims=True) - slopes[h] * dist
        ss.append(jnp.where(valid, s, NEG_BIG))
        s0s.append(jnp.sum(k_cols[h] * q_cols[h], axis=1, keepdims=True))
    ms = [jnp.maximum(jnp.max(s, axis=-1, keepdims=True), s0) for s, s0 in zip(ss, s0s)]
    ps = [jnp.exp(s - m) for s, m in zip(ss, ms)]
    p0s = [jnp.exp(s0 - m) for s0, m in zip(s0s, ms)]
    ls = [jnp.sum(p, axis=-1, keepdims=True) + p0 for p, p0 in zip(ps, p0s)]
    accs = []
    for h in heads:
        v_h = c_ref[:, 1, h]
        co_ref[:, 1, h] = jnp.where(is_newest, v_cols[h], pltpu.roll(v_h, w - 1, axis=2))
        accs.append(jnp.sum(v_h * ps[h], axis=-1, keepdims=True))
    outs = [(acc + p0 * v_col) / l for acc, p0, v_col, l in zip(accs, p0s, v_cols, ls)]
    o_ref[...] = jnp.concatenate(outs, axis=-1)
    l_ref[...] = jnp.concatenate([m + jnp.log(l) for m, l in zip(ms, ls)], axis=-1)


def _shift_attend(cache_t, new_t, prev_out, *, layer, group, block_bytes=8 * 1024 * 1024):
    n_att, b = cache_t.shape[:2]
    w = cache_t.shape[-1]
    bb = _pick_tile(b, max(1, block_bytes // (2 * SWA_WIDTH * w * 4)))
    blk = (None, bb, 2, SWA_HEADS, SWA_HEAD_DIM, w)
    in_specs = [pl.BlockSpec(blk, lambda i: (layer, i, 0, 0, 0, 0)),
                pl.BlockSpec((bb, 3, SWA_HEAD_DIM, SWA_HEADS), lambda i: (i, 0, 0, 0))]
    args = [cache_t, new_t]
    aliases = {}
    if prev_out is not None:
        in_specs.append(pl.BlockSpec(memory_space=pl.ANY))
        args.append(prev_out)
        aliases = {2: 0}
    return pl.pallas_call(
        functools.partial(_shift_attend_kernel, group=group, has_prev=prev_out is not None),
        grid=(b // bb,),
        in_specs=in_specs,
        out_specs=[pl.BlockSpec(blk, lambda i: (layer, i, 0, 0, 0, 0)),
                   pl.BlockSpec((bb, SWA_HEAD_DIM, SWA_HEADS), lambda i: (i, 0, 0)),
                   pl.BlockSpec((bb, 1, SWA_HEADS), lambda i: (i, 0, 0))],
        out_shape=[jax.ShapeDtypeStruct(cache_t.shape, cache_t.dtype),
                   jax.ShapeDtypeStruct((b, SWA_HEAD_DIM, SWA_HEADS), F32),
                   jax.ShapeDtypeStruct((b, 1, SWA_HEADS), F32)],
        input_output_aliases=aliases,
        compiler_params=_cparams(("parallel",)),
        name="swa_shift_attend_g%d" % group,
    )(*args)


def _group_merge_kernel(o0, o1, o2, l0, l1, l2, y_ref):
    a0, a1, a2 = l0[...], l1[...], l2[...]
    m = jnp.maximum(jnp.maximum(a0, a1), a2)
    e0, e1, e2 = jnp.exp(a0 - m), jnp.exp(a1 - m), jnp.exp(a2 - m)
    y_ref[...] = (e0 * o0[...] + e1 * o1[...] + e2 * o2[...]) / (e0 + e1 + e2)


def _group_merge(outs, lses):
    return pl.pallas_call(
        _group_merge_kernel,
        out_shape=jax.ShapeDtypeStruct(outs[0].shape, F32),
        name="swa_group_merge",
    )(*outs, *lses)


def _rwkv_proj_kernel(*refs, seq_mode, has_vres, tiles_per_seq):
    it = iter(refs)
    x_ref = next(it)
    prev_ref = next(it)
    vfirst_ref = next(it) if has_vres else None
    g_ref, mix_ref, wr_ref, wk_ref, wv_ref = next(it), next(it), next(it), next(it), next(it)
    w0_ref, w1_ref, w2_ref = next(it), next(it), next(it)
    a0_ref, a1_ref, a2_ref = next(it), next(it), next(it)
    if has_vres:
        v0_ref, v1_ref, v2_ref = next(it), next(it), next(it)
    g1_ref, g2_ref, kk_ref, ka_ref = next(it), next(it), next(it), next(it)
    hsel_ref, hselt_ref = next(it), next(it)
    r_out, ld_out, k_out, v_out, a_out, b_out, g_out, xn_last_out = (next(it) for _ in range(8))

    xn = _rms(x_ref[...], g_ref[...])
    tm = xn.shape[0]
    if seq_mode:
        i = pl.program_id(0)
        pr = _rms(prev_ref[...], g_ref[...])[7:8]
        pr = jnp.where(i % tiles_per_seq == 0, 0.0, pr)
        row = lax.broadcasted_iota(jnp.int32, xn.shape, 0)
        xp = jnp.where(row == 0, pr, pltpu.roll(xn, 1, axis=0))
    else:
        xp = prev_ref[...]
    xx = xp - xn
    mix = mix_ref[...]
    xr, xw, xk, xv, xa, xg = (xn + xx * mix[j:j + 1] for j in range(6))
    r = _bdot(xr, wr_ref[...])
    k = _bdot(xk, wk_ref[...])
    v = _bdot(xv, wv_ref[...])
    z = w0_ref[...] + _bdot(jnp.tanh(_bdot(xw, w1_ref[...])), w2_ref[...])
    w_log = -(jnp.maximum(-z, 0.0) + jnp.log(1.0 + jnp.exp(-jnp.abs(z)))) - 0.5
    ld = -jnp.exp(w_log)
    a = _sigmoid(a0_ref[...] + _bdot(_bdot(xa, a1_ref[...]), a2_ref[...]))
    g = _bdot(_sigmoid(_bdot(xg, g1_ref[...])), g2_ref[...])
    if has_vres:
        v = v + (vfirst_ref[...] - v) * _sigmoid(v0_ref[...] + _bdot(_bdot(xv, v1_ref[...]), v2_ref[...]))
    kk = k * kk_ref[...]
    ssq = jnp.dot((kk * kk).astype(BF16), hsel_ref[...], preferred_element_type=F32)
    rs = lax.rsqrt(jnp.maximum(ssq, 1e-24))
    kk = kk * _dot_sel(rs, hselt_ref[...])
    k = k * (1.0 + (a - 1.0) * ka_ref[...])
    r_out[...] = r
    ld_out[...] = ld
    k_out[...] = k
    v_out[...] = v
    a_out[...] = -kk
    b_out[...] = kk * a
    g_out[...] = g
    xn_last_out[...] = xn[tm - 8:tm] if seq_mode else xn


def _head_selectors(d):
    heads = d // RW_HEAD
    ch = jnp.arange(d) // RW_HEAD
    hsel = (ch[:, None] == jnp.arange(128)[None, :]).astype(BF16)
    return hsel, hsel.T


def _rwkv_proj(x, prev, v_first, p, *, seq_mode, seq, tm_target=256):
    n, d = x.shape
    tm = _pick_tile(seq if seq_mode else n, tm_target)
    has_vres = v_first is not None
    row = lambda a: a.reshape(1, -1)
    full = lambda a: pl.BlockSpec(a.shape, lambda i: (0,) * a.ndim)
    tile = pl.BlockSpec((tm, d), lambda i: (i, 0))
    args, specs = [x], [tile]
    if seq_mode:
        args.append(x)
        specs.append(pl.BlockSpec((8, d), lambda i: (jnp.maximum(i * (tm // 8) - 1, 0), 0)))
    else:
        args.append(prev)
        specs.append(tile)
    if has_vres:
        args.append(v_first)
        specs.append(tile)
    hsel, hselt = _head_selectors(d)
    consts = [row(p["norm_g"]), p["mix"], p["w_r"], p["w_k"], p["w_v"],
              row(p["w0"]), p["w1"], p["w2"], row(p["a0"]), p["a1"], p["a2"]]
    if has_vres:
        consts += [row(p["v0"]), p["v1"], p["v2"]]
    consts += [p["g1"], p["g2"], row(p["k_k"]), row(p["k_a"]), hsel, hselt]
    args += consts
    specs += [full(c) for c in consts]
    outs = pl.pallas_call(
        functools.partial(_rwkv_proj_kernel, seq_mode=seq_mode, has_vres=has_vres,
                          tiles_per_seq=(seq // tm) if seq_mode else 1),
        grid=(n // tm,),
        in_specs=specs,
        out_specs=[tile] * 7 + [pl.BlockSpec((8, d), lambda i: (i, 0)) if seq_mode else tile],
        out_shape=[jax.ShapeDtypeStruct((n, d), F32)] * 7
        + [jax.ShapeDtypeStruct((n // tm * 8 if seq_mode else n, d), F32)],
        compiler_params=_cparams(("parallel",)),
        name="rwkv_proj",
    )(*args)
    return outs


def _wkv_chunk_kernel(r_ref, ld_ref, k_ref, v_ref, a_ref, b_ref, lnw_ref, lnb_ref, rk_ref,
                      y_ref, s_out_ref, s_ref, pre_ref, mat_ref, gam_ref, *, n_chunks):
    C = WKV_CHUNK
    HD = RW_HEAD
    P = 2 * RW_HEAD
    t = pl.program_id(2)
    slot_wr = t % 2
    slot_rd = 1 - slot_wr

    @pl.when(t == 0)
    def _():
        s_ref[...] = jnp.zeros_like(s_ref)
        pre_ref[1] = jnp.zeros(pre_ref.shape[1:], F32)
        mat_ref[1] = jnp.zeros(mat_ref.shape[1:], F32)
        gam_ref[1] = jnp.zeros(gam_ref.shape[1:], F32)

    ri = lax.broadcasted_iota(jnp.int32, (C, C), 0)
    ci = lax.broadcasted_iota(jnp.int32, (C, C), 1)
    tril = ri >= ci
    stril = ri > ci
    tri_ones = tril.astype(BF16)
    first = lambda rows: lax.broadcasted_iota(jnp.int32, (rows, P), 1) < HD
    first_h, first_c, first_2c = first(HD), first(C), first(2 * C)
    block_ones = (lax.broadcasted_iota(jnp.int32, (P, P), 0) // HD
                  == lax.broadcasted_iota(jnp.int32, (P, P), 1) // HD).astype(BF16)
    lnw, lnb, rk = lnw_ref[...], lnb_ref[...], rk_ref[...]

    chunks = []
    chains = []
    for c in range(n_chunks):
        sl = slice(c * C, (c + 1) * C)
        rc, ldc, kc, vc, ac, bc = (z[sl, :] for z in (r_ref, ld_ref, k_ref, v_ref, a_ref, b_ref))
        cum = _sel_dot(tri_ones, ldc)
        e_pos = jnp.exp(cum)
        e_neg = jnp.exp(-cum)
        at = ac * jnp.exp(cum - ldc)
        rt = rc * e_pos
        bt = bc * e_neg
        kt = kc * e_neg
        last = cum[C - 1:C]
        e_last = jnp.exp(last - cum)
        bh = bc * e_last
        kh = kc * e_last
        gamma = jnp.exp(last)
        pre_ref[slot_wr, c, 2] = _bdot(rc * kc * rk, block_ones) * vc
        gam_ref[slot_wr, c] = jnp.broadcast_to(gamma, (8, P))
        chunk = dict(at_swapped=pltpu.roll(at, HD, axis=1), rt=rt, v=vc,
                     ar=jnp.concatenate([at, rt], axis=0), bk=jnp.concatenate([bt, kt], axis=0),
                     bkh=jnp.concatenate([bh, kh], axis=0))
        chunks.append(chunk)
        chains += [dict(h=0, chunk=chunk), dict(h=1, chunk=chunk)]

    def stage_scores():
        for ch in chains:
            ck = ch["chunk"]
            mine = first_2c if ch["h"] == 0 else ~first_2c
            sc = _bdot_nt(jnp.where(mine, ck["ar"], 0.0), ck["bk"])
            ch["lp"] = jnp.where(stril, sc[:C, :C], 0.0)
            ch["ak"] = jnp.where(stril, sc[:C, C:], 0.0)
            ch["rbk"] = jnp.concatenate([jnp.where(tril, sc[C:, :C], 0.0),
                                         jnp.where(tril, sc[C:, C:], 0.0)], axis=1)

    def stage_x0():
        for ch in chains:
            ck = ch["chunk"]
            akv = _bdot(ch["ak"], ck["v"])
            ch["x"] = (jnp.where(first_c, akv, ck["at_swapped"]) if ch["h"] == 0
                       else jnp.where(first_c, ck["at_swapped"], akv))

    def stage_level(square):
        nxt = [ch["x"] + _bdot(ch["lp"], ch["x"]) for ch in chains]
        if square:
            for ch in chains:
                ch["lp"] = _bdot(ch["lp"], ch["lp"])
        for ch, x in zip(chains, nxt):
            ch["x"] = x

    def stage_final_dots():
        for ch in chains:
            ck = ch["chunk"]
            vz = jnp.where(first_c, ck["v"], 0.0) if ch["h"] == 0 else jnp.where(first_c, 0.0, ck["v"])
            z = jnp.concatenate([ch["x"], vz], axis=0)
            ch["ry"] = _bdot(ch["rbk"], z)
            ch["qh"] = _bdot_tn(z, ck["bkh"])

    def stage_final_stores():
        for c, ck in enumerate(chunks):
            ry_a, ry_b = chains[2 * c]["ry"], chains[2 * c + 1]["ry"]
            qa, qb = chains[2 * c]["qh"], chains[2 * c + 1]["qh"]
            pre_ref[slot_wr, c, 0] = ck["rt"] + pltpu.roll(jnp.where(first_c, ry_b, ry_a), HD, axis=1)
            pre_ref[slot_wr, c, 1] = jnp.where(first_c, ry_a, ry_b)
            mat_ref[slot_wr, c, 0] = jnp.concatenate([jnp.where(first_h, qa[HD:], 0.0),
                                                      jnp.where(first_h, 0.0, qb[:HD])], axis=0)
            mat_ref[slot_wr, c, 1] = jnp.concatenate([jnp.where(first_h, qa[:HD], 0.0),
                                                      jnp.where(first_h, 0.0, qb[HD:])], axis=0)

    state = [s_ref[...]]

    def state_step(c):
        s0 = state[0]
        y = _bdot_nt(pre_ref[slot_rd, c, 0], s0) + pre_ref[slot_rd, c, 1]
        state[0] = s0 * gam_ref[slot_rd, c, 0:1] + _bdot(s0, mat_ref[slot_rd, c, 0]) + mat_ref[slot_rd, c, 1]
        mu = _bdot(y, block_ones) * (1.0 / HD)
        yc = y - mu
        var = _bdot(yc * yc, block_ones) * (1.0 / HD)
        y_ref[c * C:(c + 1) * C, :] = yc * lax.rsqrt(var + RW_LN_EPS) * lnw + lnb + pre_ref[slot_rd, c, 2]

    stages = [stage_scores, stage_x0]
    n = 1
    while n < C:
        stages.append(functools.partial(stage_level, 2 * n < C))
        n *= 2
    stages += [stage_final_dots, stage_final_stores]
    for si, stage in enumerate(stages):
        stage()
        for c in range(si * n_chunks // len(stages), (si + 1) * n_chunks // len(stages)):
            state_step(c)
    s_ref[...] = state[0]

    @pl.when(t == pl.num_programs(2) - 1)
    def _():
        s_out_ref[0] = s_ref[0:HD, 0:HD]
        s_out_ref[1] = s_ref[HD:P, HD:P]


def _wkv_chunked(r, ld, k, v, a, b, ln_w, ln_b, r_k, *, bsz, seq, tb_target=512):
    n, d = r.shape
    heads = d // RW_HEAD
    pairs = heads // 2
    tb = _pick_tile(seq, tb_target)
    per_seq = seq // tb
    n_chunks = tb // WKV_CHUNK
    in_tile = pl.BlockSpec((tb, 128), lambda bi, hp, t: (bi * per_seq + jnp.minimum(t, per_seq - 1), hp))
    out_tile = pl.BlockSpec((tb, 128), lambda bi, hp, t: (bi * per_seq + jnp.maximum(t - 1, 0), hp))
    vec = pl.BlockSpec((1, 128), lambda bi, hp, t: (0, hp))
    y, st = pl.pallas_call(
        functools.partial(_wkv_chunk_kernel, n_chunks=n_chunks),
        grid=(bsz, pairs, per_seq + 1),
        in_specs=[in_tile] * 6 + [vec] * 3,
        out_specs=[out_tile, pl.BlockSpec((None, 2, RW_HEAD, RW_HEAD), lambda bi, hp, t: (bi, hp, 0, 0))],
        out_shape=[jax.ShapeDtypeStruct((n, d), F32),
                   jax.ShapeDtypeStruct((bsz, heads, RW_HEAD, RW_HEAD), F32)],
        scratch_shapes=[pltpu.VMEM((128, 128), F32),
                        pltpu.VMEM((2, n_chunks, 3, WKV_CHUNK, 128), F32),
                        pltpu.VMEM((2, n_chunks, 2, 128, 128), F32),
                        pltpu.VMEM((2, n_chunks, 8, 128), F32)],
        compiler_params=_cparams(("parallel", "parallel", "arbitrary")),
        name="wkv_chunked",
    )(r, ld, k, v, a, b, ln_w.reshape(1, d), ln_b.reshape(1, d), r_k.reshape(1, d))
    return y, st


def _wkv_step_kernel(*refs, has_prev):
    it = iter(refs)
    s_ref, r_ref, ld_ref, k_ref, a_ref, b_ref, rk_ref, v_ref, lnw_ref, lnb_ref = (next(it) for _ in range(10))
    if has_prev:
        next(it)
    s_out, y_out = next(it), next(it)
    s = s_ref[...]
    r, k = r_ref[...], k_ref[...]
    v = v_ref[...]
    sa = jnp.sum(s * a_ref[...][:, None], axis=2, keepdims=True)
    s_new = s * jnp.exp(ld_ref[...])[:, None] + sa * b_ref[...][:, None] + v * k[:, None]
    y = jnp.sum(s_new * r[:, None], axis=2, keepdims=True)
    mu = jnp.mean(y, axis=1, keepdims=True)
    yc = y - mu
    var = jnp.mean(yc * yc, axis=1, keepdims=True)
    yn = yc * lax.rsqrt(var + RW_LN_EPS) * lnw_ref[...] + lnb_ref[...]
    bonus = jnp.sum(r * k * rk_ref[...], axis=1, keepdims=True)[:, None] * v
    s_out[...] = s_new
    y_out[...] = yn + bonus


def _wkv_step(state_t, prev_out, layer, r, ld, k, v, a, b, ln_w, ln_b, r_k, *, hb=2):
    heads, bsz = state_t.shape[1], state_t.shape[-1]
    d = heads * RW_HEAD
    per_key = lambda z: z.T.reshape(heads, RW_HEAD, bsz)
    per_val = lambda z: z.T.reshape(heads, RW_HEAD, 1, bsz)
    bcast = lambda p, shape: jnp.broadcast_to(p.reshape(shape[:-1] + (1,)), shape)
    st_spec = pl.BlockSpec((None, hb, RW_HEAD, RW_HEAD, bsz), lambda i: (layer, i, 0, 0, 0))
    key_spec = pl.BlockSpec((hb, RW_HEAD, bsz), lambda i: (i, 0, 0))
    val_spec = pl.BlockSpec((hb, RW_HEAD, 1, bsz), lambda i: (i, 0, 0, 0))
    in_specs = [st_spec] + [key_spec] * 6 + [val_spec] * 3
    args = [state_t, per_key(r), per_key(ld), per_key(k), per_key(a), per_key(b),
            bcast(r_k, (heads, RW_HEAD, bsz)), per_val(v),
            bcast(ln_w, (heads, RW_HEAD, 1, bsz)), bcast(ln_b, (heads, RW_HEAD, 1, bsz))]
    aliases = {}
    if prev_out is not None:
        in_specs.append(pl.BlockSpec(memory_space=pl.ANY))
        args.append(prev_out)
        aliases = {10: 0}
    s_new, y = pl.pallas_call(
        functools.partial(_wkv_step_kernel, has_prev=prev_out is not None),
        grid=(heads // hb,),
        in_specs=in_specs,
        out_specs=[st_spec, val_spec],
        out_shape=[jax.ShapeDtypeStruct(state_t.shape, F32),
                   jax.ShapeDtypeStruct((heads, RW_HEAD, 1, bsz), F32)],
        input_output_aliases=aliases,
        compiler_params=_cparams(("parallel",)),
        name="wkv_step",
    )(*args)
    return y.reshape(d, bsz).T, s_new


def _run_layers(x, mem_kv, wkv0, shift0, swa_bufs, W, *, bsz, seq):
    is_prompt = swa_bufs is None
    depth = W["norm_g"].shape[0]
    new_wkv, new_shift = [], []
    new_kv = [[] for _ in range(N_GROUPS)]
    v_first = None
    for l in range(depth):
        ng = W["norm_g"][l]
        x = _ffn(x, ng[0], W["ffn_w_in"], W["ffn_w_out"], l, 0)
        i = l // 2
        if l % 2 == 0:
            p = dict(norm_g=ng[1], mix=W["rw_mix"][i], w_r=W["rw_w_rkv"][i, 0], w_k=W["rw_w_rkv"][i, 1],
                     w_v=W["rw_w_rkv"][i, 2], w0=W["rw_w0"][i], w1=W["rw_w1"][i], w2=W["rw_w2"][i],
                     a0=W["rw_a0"][i], a1=W["rw_a1"][i], a2=W["rw_a2"][i], g1=W["rw_g1"][i], g2=W["rw_g2"][i],
                     k_k=W["rw_k_k"][i], k_a=W["rw_k_a"][i])
            if i > 0:
                p.update(v0=W["rw_v0"][i - 1], v1=W["rw_v1"][i - 1], v2=W["rw_v2"][i - 1])
            prev = None if is_prompt else shift0[i]
            r, ld, k, v, a, b, g, xn_last = _rwkv_proj(x, prev, v_first if i > 0 else None, p,
                                                      seq_mode=is_prompt, seq=seq)
            if i == 0:
                v_first = v
            if is_prompt:
                y, s_new = _wkv_chunked(r, ld, k, v, a, b, W["rw_ln_w"][i], W["rw_ln_b"][i],
                                        W["rw_r_k"][i].reshape(-1), bsz=bsz, seq=seq)
                tiles = xn_last.shape[0] // 8 // bsz
                shift_new = xn_last.reshape(bsz, tiles, 8, -1)[:, -1, -1]
                new_wkv.append(s_new)
            else:
                y, s_new = _wkv_step(wkv0, new_wkv[0] if new_wkv else None, i, r, ld, k, v, a, b,
                                     W["rw_ln_w"][i], W["rw_ln_b"][i], W["rw_r_k"][i].reshape(-1))
                shift_new = xn_last
                new_wkv = [s_new]
            new_shift.append(shift_new)
            x = _linear(y, W["rw_w_o"][i], mul=g, res=x, name="rwkv_out")
        else:
            outs, lses = [], []
            if is_prompt:
                qkv_groups = _swa_qkv_prompt(x, ng[1], W["swa_w_qkv"], i, bsz=bsz, seq=seq)
                for gi in range(N_GROUPS):
                    o, lse = _band_attention(qkv_groups[gi], gi)
                    outs.append(o)
                    lses.append(lse)
                    dil = SWA_DILATIONS[gi]
                    keep = min(SWA_WINDOWS[gi], seq)
                    kv_tail = qkv_groups[gi][:, :, (seq - keep) // dil:, SWA_WIDTH:]
                    new_kv[gi].append(jnp.swapaxes(kv_tail, 1, 2).reshape(
                        bsz, keep, 2, SWA_HEADS, SWA_HEAD_DIM))
                x = _swa_merge(outs, lses, x, W["swa_w_o"][i], seq=seq)
            else:
                qkv = _linear(x, W["swa_w_qkv"][i], norm_g=ng[1], tn_target=512, name="swa_qkv")
                qkv5 = qkv.reshape(bsz, N_GROUPS, 3, SWA_HEADS, SWA_HEAD_DIM)
                for gi in range(N_GROUPS):
                    prev_out = new_kv[gi][0] if new_kv[gi] else None
                    new_t = jnp.swapaxes(qkv5[:, gi], -1, -2)
                    cache_out, o, lse = _shift_attend(swa_bufs[gi], new_t, prev_out, layer=i, group=gi)
                    new_kv[gi] = [cache_out]
                    outs.append(o)
                    lses.append(lse)
                y = jnp.swapaxes(_group_merge(outs, lses), 1, 2).reshape(bsz, SWA_WIDTH)
                x = _linear(y, W["swa_w_o"][i], res=x, name="swa_out")
        if is_prompt:
            x = _xa_prompt(x, ng[2], W["xa_w_q"][l], mem_kv[l], W["xa_w_o"][l], seq=seq)
        else:
            q = _linear(x, W["xa_w_q"][l], norm_g=ng[2], name="xa_q")
            o = _xa_sample(q, mem_kv, l)
            x = _linear(o, W["xa_w_o"][l], res=x, name="xa_out")
        x = _ffn(x, ng[3], W["ffn_w_in"], W["ffn_w_out"], l, 1,
                 final_g=W["final_norm_g"] if l == depth - 1 else None)
    return x, new_wkv, new_shift, new_kv


def kernel(x_prompt, x_sample, cache_mem_kv, state_rwkv_wkv, state_rwkv_shift, cache_swa_kv_g0, cache_swa_kv_g1, cache_swa_kv_g2, mem_prompt, norm_g, mem_norm_g, final_norm_g, ffn_w_in, ffn_w_out, rw_mix, rw_w_rkv, rw_w_o, rw_w0, rw_w1, rw_w2, rw_a0, rw_a1, rw_a2, rw_v0, rw_v1, rw_v2, rw_g1, rw_g2, rw_k_k, rw_k_a, rw_r_k, rw_ln_w, rw_ln_b, swa_w_qkv, swa_w_o, xa_w_q, xa_w_kv, xa_w_o):
    bf = lambda w: w.astype(BF16)
    W = dict(norm_g=norm_g, final_norm_g=final_norm_g, ffn_w_in=bf(ffn_w_in), ffn_w_out=bf(ffn_w_out),
             rw_mix=rw_mix, rw_w_rkv=bf(rw_w_rkv), rw_w_o=bf(rw_w_o), rw_w0=rw_w0, rw_w1=bf(rw_w1),
             rw_w2=bf(rw_w2), rw_a0=rw_a0, rw_a1=bf(rw_a1), rw_a2=bf(rw_a2), rw_v0=rw_v0, rw_v1=bf(rw_v1),
             rw_v2=bf(rw_v2), rw_g1=bf(rw_g1), rw_g2=bf(rw_g2), rw_k_k=rw_k_k, rw_k_a=rw_k_a, rw_r_k=rw_r_k,
             rw_ln_w=rw_ln_w, rw_ln_b=rw_ln_b, swa_w_qkv=bf(swa_w_qkv), swa_w_o=bf(swa_w_o),
             xa_w_q=bf(xa_w_q), xa_w_o=bf(xa_w_o))
    depth = norm_g.shape[0]
    bsz, seq, d = x_prompt.shape
    n_mem = mem_prompt.shape[1]

    mem2 = mem_prompt.reshape(bsz * n_mem, d)
    xa_w_kv_b = bf(xa_w_kv)
    mem_kv_p = jnp.stack([_linear(mem2, xa_w_kv_b[l], norm_g=mem_norm_g[l], name="mem_kv")
                          for l in range(depth)])
    y_p, wkv_p, shift_p, kv_p = _run_layers(
        x_prompt.reshape(bsz * seq, d), mem_kv_p.reshape(depth, bsz, n_mem, 2 * XA_WIDTH),
        None, None, None, W, bsz=bsz, seq=seq)

    dbsz, dseq, _ = x_sample.shape
    assert dseq == 1
    caches = tuple(jnp.transpose(c, (0, 1, 3, 4, 5, 2))
                   for c in (cache_swa_kv_g0, cache_swa_kv_g1, cache_swa_kv_g2))
    wkv_t = jnp.transpose(state_rwkv_wkv, (0, 2, 3, 4, 1))
    y_s, wkv_s, shift_s, kv_s = _run_layers(
        x_sample.reshape(dbsz, d), cache_mem_kv,
        wkv_t, state_rwkv_shift, caches, W, bsz=dbsz, seq=1)
    swa_s = tuple(jnp.transpose(kv_s[g][0], (0, 1, 5, 2, 3, 4)) for g in range(N_GROUPS))
    wkv_s = jnp.transpose(wkv_s[0], (0, 4, 1, 2, 3))

    return (y_p.reshape(bsz, seq, d), y_s.reshape(dbsz, 1, d),
            mem_kv_p.reshape(depth, bsz, n_mem, 2, XA_HEADS, XA_HEAD_DIM),
            jnp.stack(wkv_p), jnp.stack(shift_p),
            jnp.stack(kv_p[0]), jnp.stack(kv_p[1]), jnp.stack(kv_p[2]),
            wkv_s, jnp.stack(shift_s), swa_s[0], swa_s[1], swa_s[2])
```

```python
import functools

import jax
import jax.numpy as jnp
from jax import lax
from jax.experimental import pallas as pl
from jax.experimental.pallas import tpu as pltpu

F32 = jnp.float32
BF16 = jnp.bfloat16

NORM_EPS = 1e-6
RW_HEAD = 64
RW_LN_EPS = RW_HEAD * 1e-5
SWA_WINDOWS = (128, 512, 2048)
SWA_DILATIONS = (1, 4, 16)
N_GROUPS = 3
SWA_HEADS = 8
SWA_HEAD_DIM = 64
SWA_WIDTH = SWA_HEADS * SWA_HEAD_DIM
SWA_BLOCK = 128
XA_HEADS = 4
XA_HEAD_DIM = 128
XA_WIDTH = XA_HEADS * XA_HEAD_DIM
WKV_CHUNK = 64
NEG_BIG = -1e30
VMEM_LIMIT = 56 * 1024 * 1024


def _cparams(sem):
    return pltpu.CompilerParams(dimension_semantics=sem, vmem_limit_bytes=VMEM_LIMIT)


def _bdot(a, b):
    return jnp.dot(a.astype(BF16), b.astype(BF16), preferred_element_type=F32)


def _bdot_nt(a, b):
    return lax.dot_general(a.astype(BF16), b.astype(BF16), (((1,), (1,)), ((), ())),
                           preferred_element_type=F32)


def _bdot_tn(a, b):
    return lax.dot_general(a.astype(BF16), b.astype(BF16), (((0,), (0,)), ((), ())),
                           preferred_element_type=F32)


def _split3(x):
    h1 = x.astype(BF16)
    r1 = x - h1.astype(F32)
    h2 = r1.astype(BF16)
    h3 = (r1 - h2.astype(F32)).astype(BF16)
    return h1, h2, h3


def _sel_dot(sel_bf16, x):
    h1, h2, h3 = _split3(x)
    d = lambda h: jnp.dot(sel_bf16, h, preferred_element_type=F32)
    return d(h1) + d(h2) + d(h3)


def _dot_sel(x, sel_bf16):
    h1, h2, h3 = _split3(x)
    d = lambda h: jnp.dot(h, sel_bf16, preferred_element_type=F32)
    return d(h1) + d(h2) + d(h3)


def _rms(x, g):
    return x * lax.rsqrt(jnp.mean(x * x, axis=-1, keepdims=True) + NORM_EPS) * g


def _sigmoid(x):
    return 1.0 / (1.0 + jnp.exp(-x))


def _param(arr, *idx):
    blk = (None,) * len(idx) + tuple(arr.shape[len(idx):])
    tail = (0,) * (arr.ndim - len(idx))
    return arr, pl.BlockSpec(blk, lambda *_: tuple(idx) + tail)


def _pick_tile(n, target):
    t = min(n, target)
    while n % t:
        t //= 2
    return t


def _ffn_kernel(*refs, final_norm):
    if final_norm:
        x_ref, g_ref, wg_ref, wu_ref, wo_ref, fg_ref, o_ref, xn_ref, acc_ref = refs
    else:
        x_ref, g_ref, wg_ref, wu_ref, wo_ref, o_ref, xn_ref, acc_ref = refs
    j = pl.program_id(1)

    @pl.when(j == 0)
    def _():
        xn_ref[...] = _rms(x_ref[...], g_ref[...]).astype(BF16)
        acc_ref[...] = jnp.zeros_like(acc_ref)

    xn = xn_ref[...]
    gate = jnp.dot(xn, wg_ref[...], preferred_element_type=F32)
    up = jnp.dot(xn, wu_ref[...], preferred_element_type=F32)
    h = (gate * _sigmoid(gate) * up).astype(BF16)
    acc_ref[...] += jnp.dot(h, wo_ref[...], preferred_element_type=F32)

    @pl.when(j == pl.num_programs(1) - 1)
    def _():
        y = x_ref[...] + 0.5 * acc_ref[...]
        o_ref[...] = _rms(y, fg_ref[...]) if final_norm else y


def _ffn(x, g, w_in, w_out, layer, which, *, final_g=None, tm_target=512, tf_target=1408):
    n, d = x.shape
    d_ff = w_out.shape[2]
    tm = _pick_tile(n, tm_target)
    tf = tf_target if d_ff % tf_target == 0 else 128
    nf = d_ff // tf
    in_specs = [
        pl.BlockSpec((tm, d), lambda i, j: (i, 0)),
        g[1],
        pl.BlockSpec((None, None, d, tf), lambda i, j: (layer, which, 0, j)),
        pl.BlockSpec((None, None, d, tf), lambda i, j: (layer, which, 0, j + nf)),
        pl.BlockSpec((None, None, tf, d), lambda i, j: (layer, which, j, 0)),
    ]
    args = [x, g[0], w_in, w_in, w_out]
    if final_g is not None:
        in_specs.append(final_g[1])
        args.append(final_g[0])
    return pl.pallas_call(
        functools.partial(_ffn_kernel, final_norm=final_g is not None),
        grid=(n // tm, nf),
        in_specs=in_specs,
        out_specs=pl.BlockSpec((tm, d), lambda i, j: (i, 0)),
        out_shape=jax.ShapeDtypeStruct((n, d), F32),
        scratch_shapes=[pltpu.VMEM((tm, d), BF16), pltpu.VMEM((tm, d), F32)],
        compiler_params=_cparams(("parallel", "arbitrary")),
        name="ffn",
    )(*args)


def _linear_kernel(*refs, mode, has_res):
    it = iter(refs)
    x_ref = next(it)
    p_ref = next(it) if mode in ("norm", "mul") else None
    w_ref = next(it)
    res_ref = next(it) if has_res else None
    o_ref = next(it)
    xs_ref = next(it)

    @pl.when(pl.program_id(1) == 0)
    def _():
        x = x_ref[...]
        if mode == "norm":
            x = _rms(x, p_ref[...])
        elif mode == "mul":
            x = x * p_ref[...]
        xs_ref[...] = x.astype(BF16)

    o = jnp.dot(xs_ref[...], w_ref[...], preferred_element_type=F32)
    if has_res:
        o = res_ref[...] + o
    o_ref[...] = o


def _linear(x, w, layer, *, norm_g=None, mul=None, res=None, tm_target=512, tn_target=1024, name="linear"):
    n, kd = x.shape
    nout = w.shape[2]
    tm = _pick_tile(n, tm_target)
    tn = _pick_tile(nout, tn_target)
    mode = "norm" if norm_g is not None else ("mul" if mul is not None else "plain")
    args = [x]
    specs = [pl.BlockSpec((tm, kd), lambda i, j: (i, 0))]
    if mode == "norm":
        args.append(norm_g[0])
        specs.append(norm_g[1])
    elif mode == "mul":
        args.append(mul)
        specs.append(pl.BlockSpec((tm, kd), lambda i, j: (i, 0)))
    args.append(w)
    specs.append(pl.BlockSpec((None, kd, tn), lambda i, j: (layer, 0, j)))
    if res is not None:
        args.append(res)
        specs.append(pl.BlockSpec((tm, tn), lambda i, j: (i, j)))
    return pl.pallas_call(
        functools.partial(_linear_kernel, mode=mode, has_res=res is not None),
        grid=(n // tm, nout // tn),
        in_specs=specs,
        out_specs=pl.BlockSpec((tm, tn), lambda i, j: (i, j)),
        out_shape=jax.ShapeDtypeStruct((n, nout), F32),
        scratch_shapes=[pltpu.VMEM((tm, kd), BF16)],
        compiler_params=_cparams(("parallel", "arbitrary")),
        name=name,
    )(*args)


def _xa_prompt_kernel(x_ref, g_ref, wq_ref, kv_ref, wo_ref, o_ref):
    x = x_ref[...]
    q = jnp.dot(_rms(x, g_ref[...]).astype(BF16), wq_ref[...], preferred_element_type=F32)
    kv = kv_ref[...].astype(BF16)
    outs = []
    for h in range(XA_HEADS):
        lo = h * XA_HEAD_DIM
        s = _bdot_nt(q[:, lo:lo + XA_HEAD_DIM], kv[:, lo:lo + XA_HEAD_DIM]) * (XA_HEAD_DIM ** -0.5)
        m = jnp.max(s, axis=-1, keepdims=True)
        p = jnp.exp(s - m)
        l = jnp.sum(p, axis=-1, keepdims=True)
        o = jnp.dot(p.astype(BF16), kv[:, XA_WIDTH + lo:XA_WIDTH + lo + XA_HEAD_DIM],
                    preferred_element_type=F32)
        outs.append(o / l)
    o = jnp.concatenate(outs, axis=-1)
    o_ref[...] = x + jnp.dot(o.astype(BF16), wo_ref[...], preferred_element_type=F32)


def _xa_prompt(x, g, wq, kv, wo, layer, *, seq, tm_target=512):
    n, d = x.shape
    n_mem = kv.shape[2]
    tm = _pick_tile(seq, tm_target)
    per_b = seq // tm
    return pl.pallas_call(
        _xa_prompt_kernel,
        grid=(n // tm,),
        in_specs=[
            pl.BlockSpec((tm, d), lambda i: (i, 0)),
            g[1],
            wq[1],
            pl.BlockSpec((None, None, n_mem, 2 * XA_WIDTH), lambda i: (layer, i // per_b, 0, 0)),
            wo[1],
        ],
        out_specs=pl.BlockSpec((tm, d), lambda i: (i, 0)),
        out_shape=jax.ShapeDtypeStruct((n, d), F32),
        compiler_params=_cparams(("parallel",)),
        name="xa_prompt",
    )(x, g[0], wq[0], kv, wo[0])


def _xa_sample_kernel(q_ref, kv_ref, o_ref):
    q = q_ref[...]
    k = kv_ref[:, :, 0]
    v = kv_ref[:, :, 1]
    s = jnp.sum(k * q[:, None], axis=-1, keepdims=True) * (XA_HEAD_DIM ** -0.5)
    m = jnp.max(s, axis=1, keepdims=True)
    p = jnp.exp(s - m)
    l = jnp.sum(p, axis=1)
    o_ref[...] = jnp.sum(p * v, axis=1) / l


def _xa_sample(q, kv, layer, *, bb=8):
    b = q.shape[0]
    n_mem = kv.shape[2]
    bb = _pick_tile(b, bb)
    hd = (XA_HEADS, XA_HEAD_DIM)
    out = pl.pallas_call(
        _xa_sample_kernel,
        grid=(b // bb,),
        in_specs=[
            pl.BlockSpec((bb,) + hd, lambda i: (i, 0, 0)),
            pl.BlockSpec((None, bb, n_mem, 2) + hd, lambda i: (layer, i, 0, 0, 0, 0)),
        ],
        out_specs=pl.BlockSpec((bb,) + hd, lambda i: (i, 0, 0)),
        out_shape=jax.ShapeDtypeStruct((b,) + hd, F32),
        compiler_params=_cparams(("parallel",)),
        name="xa_sample",
    )(q.reshape((b,) + hd), kv)
    return out.reshape(b, XA_WIDTH)


def _band_bias(group):
    dil = SWA_DILATIONS[group]
    units = SWA_WINDOWS[group] // dil
    qi = jnp.arange(SWA_BLOCK)[:, None]
    ki = jnp.arange(2 * SWA_BLOCK)[None, :]
    delta = qi + SWA_BLOCK - ki
    band = (delta >= 0) & (delta <= units)
    heads = jnp.arange(SWA_HEADS, dtype=F32)
    slopes = jnp.exp2(-8.0 * (group * SWA_HEADS + heads + 1.0) / (N_GROUPS * SWA_HEADS))
    alibi = -slopes[:, None, None] * (dil * delta).astype(F32)[None]
    later = jnp.where(band[None], alibi, NEG_BIG)
    first = jnp.where((band & (ki >= SWA_BLOCK))[None], alibi, NEG_BIG)
    return jnp.stack([first, later])


def _band_kernel(q_ref, kp_ref, kc_ref, vp_ref, vc_ref, bias_ref, o_ref, l_ref):
    n = pl.program_id(2)
    blocks = q_ref.shape[0] // SWA_BLOCK
    q = (q_ref[...] * (SWA_HEAD_DIM ** -0.5)).astype(BF16)
    k = jnp.concatenate([kp_ref[...], kc_ref[...]], axis=0).astype(BF16)
    v = jnp.concatenate([vp_ref[...], vc_ref[...]], axis=0).astype(BF16)
    first_of_seq = jnp.minimum(n, 1)
    work = [(sb, h) for sb in range(blocks) for h in range(SWA_HEADS)]
    rows = lambda sb: slice(sb * SWA_BLOCK, (sb + 1) * SWA_BLOCK)
    keys = lambda sb: slice(sb * SWA_BLOCK, (sb + 2) * SWA_BLOCK)
    lanes = lambda h: slice(h * SWA_HEAD_DIM, (h + 1) * SWA_HEAD_DIM)
    ss = [_bdot_nt(q[rows(sb), lanes(h)], k[keys(sb), lanes(h)])
          + (bias_ref[first_of_seq, h] if sb == 0 else bias_ref[1, h]) for sb, h in work]
    ms = [jnp.max(s, axis=-1, keepdims=True) for s in ss]
    ps = [jnp.exp(s - m) for s, m in zip(ss, ms)]
    ls = [jnp.sum(p, axis=-1, keepdims=True) for p in ps]
    os = [jnp.dot(p.astype(BF16), v[keys(sb), lanes(h)], preferred_element_type=F32)
          for p, (sb, h) in zip(ps, work)]
    for (sb, h), o, m, l in zip(work, os, ms, ls):
        o_ref[rows(sb), lanes(h)] = o / l
        l_ref[rows(sb), lanes(h)] = jnp.broadcast_to(m + jnp.log(l), (SWA_BLOCK, SWA_HEAD_DIM))


def _swa_qkv_kernel(x_ref, g_ref, w_ref, o0_ref, o1_ref, o2_ref, tmp_ref):
    xs = _rms(x_ref[...], g_ref[...]).astype(BF16)
    tm = xs.shape[0]
    gw = 3 * SWA_WIDTH
    for g, o_ref in enumerate((o0_ref, o1_ref, o2_ref)):
        dil = SWA_DILATIONS[g]
        out = jnp.dot(xs, w_ref[:, g * gw:(g + 1) * gw], preferred_element_type=F32)
        if dil == 1:
            o_ref[0] = out
        else:
            for cb in range(gw // 128):
                tmp_ref[cb] = out[:, cb * 128:(cb + 1) * 128]
            for r in range(dil):
                for cb in range(gw // 128):
                    o_ref[r, :, cb * 128:(cb + 1) * 128] = tmp_ref[cb, pl.ds(r, tm // dil, stride=dil), :]


def _swa_qkv_prompt(x, g, w, layer, *, bsz, seq, tm_target=512):
    n, d = x.shape
    tm = _pick_tile(seq, tm_target)
    per_b = seq // tm
    gw = 3 * SWA_WIDTH
    out_specs, out_shapes = [], []
    for dil in SWA_DILATIONS:
        out_specs.append(pl.BlockSpec((None, dil, tm // dil, gw), lambda i: (i // per_b, 0, i % per_b, 0)))
        out_shapes.append(jax.ShapeDtypeStruct((bsz, dil, seq // dil, gw), F32))
    return pl.pallas_call(
        _swa_qkv_kernel,
        grid=(n // tm,),
        in_specs=[pl.BlockSpec((tm, d), lambda i: (i, 0)),
                  g[1],
                  pl.BlockSpec((None,) + w.shape[1:], lambda i: (layer, 0, 0),
                               pipeline_mode=pl.Buffered(1))],
        out_specs=out_specs,
        out_shape=out_shapes,
        scratch_shapes=[pltpu.VMEM((gw // 128, tm, 128), F32)],
        compiler_params=_cparams(("parallel",)),
        name="swa_qkv_prompt",
    )(x, g[0], w)


def _band_attention(qkv_g, group):
    bsz, dil, l_sub, _ = qkv_g.shape
    per_step = 2 if l_sub % (2 * SWA_BLOCK) == 0 else 1
    rows = per_step * SWA_BLOCK
    blk = (None, None, rows, SWA_WIDTH)
    cur = lambda col: pl.BlockSpec(blk, lambda b, r, n: (b, r, n, col))
    prev = lambda col: pl.BlockSpec((None, None, SWA_BLOCK, SWA_WIDTH),
                                    lambda b, r, n: (b, r, jnp.maximum(n * per_step - 1, 0), col))
    out_spec = pl.BlockSpec(blk, lambda b, r, n: (b, r, n, 0))
    bias_spec = pl.BlockSpec((2, SWA_HEADS, SWA_BLOCK, 2 * SWA_BLOCK), lambda b, r, n: (0, 0, 0, 0))
    return pl.pallas_call(
        _band_kernel,
        grid=(bsz, dil, l_sub // rows),
        in_specs=[cur(0), prev(1), cur(1), prev(2), cur(2), bias_spec],
        out_specs=[out_spec, out_spec],
        out_shape=[jax.ShapeDtypeStruct((bsz, dil, l_sub, SWA_WIDTH), F32)] * 2,
        compiler_params=_cparams(("parallel", "parallel", "arbitrary")),
        name="swa_band_g%d" % group,
    )(qkv_g, qkv_g, qkv_g, qkv_g, qkv_g, _band_bias(group))


def _merge_kernel(o0, o1, o2, l0, l1, l2, x_ref, w_ref, out_ref, *tmp_refs):
    tm = x_ref.shape[0]
    vals = [o0[0], l0[0]]
    tmp = iter(tmp_refs)
    for g, (o_ref, l_ref) in ((1, (o1, l1)), (2, (o2, l2))):
        dil = SWA_DILATIONS[g]
        for ref in (o_ref, l_ref):
            t_ref = next(tmp)
            for r in range(dil):
                for cb in range(SWA_WIDTH // 128):
                    t_ref[cb, pl.ds(r, tm // dil, stride=dil), :] = ref[r, :, cb * 128:(cb + 1) * 128]
            vals.append(jnp.concatenate([t_ref[cb] for cb in range(SWA_WIDTH // 128)], axis=-1))
    o_0, a0, o_1, a1, o_2, a2 = vals
    m = jnp.maximum(jnp.maximum(a0, a1), a2)
    e0, e1, e2 = jnp.exp(a0 - m), jnp.exp(a1 - m), jnp.exp(a2 - m)
    y = (e0 * o_0 + e1 * o_1 + e2 * o_2) / (e0 + e1 + e2)
    out_ref[...] = x_ref[...] + jnp.dot(y.astype(BF16), w_ref[...], preferred_element_type=F32)


def _swa_merge(outs, lses, x, wo, *, seq, tm_target=256):
    n, d = x.shape
    tm = _pick_tile(seq, tm_target)
    per_b = seq // tm
    specs = [pl.BlockSpec((None, dil, tm // dil, SWA_WIDTH), lambda i: (i // per_b, 0, i % per_b, 0))
             for dil in SWA_DILATIONS]
    return pl.pallas_call(
        _merge_kernel,
        grid=(n // tm,),
        in_specs=specs + specs + [pl.BlockSpec((tm, d), lambda i: (i, 0)),
                                  wo[1]],
        out_specs=pl.BlockSpec((tm, d), lambda i: (i, 0)),
        out_shape=jax.ShapeDtypeStruct((n, d), F32),
        scratch_shapes=[pltpu.VMEM((SWA_WIDTH // 128, tm, 128), F32)] * 4,
        compiler_params=_cparams(("parallel",)),
        name="swa_merge",
    )(*outs, *lses, x, wo[0])


def _shift_attend_kernel(*refs, group, has_prev):
    it = iter(refs)
    c_ref, new_ref = next(it), next(it)
    if has_prev:
        next(it)
    co_ref, o_ref, l_ref = next(it), next(it), next(it)
    dil = SWA_DILATIONS[group]
    w = c_ref.shape[-1]
    pos = lax.broadcasted_iota(jnp.int32, (1, 1, w), 2)
    valid = (pos % dil) == 0
    dist = (w - pos).astype(F32)
    is_newest = pos == w - 1
    heads = range(SWA_HEADS)
    slopes = [2.0 ** (-8.0 * (group * SWA_HEADS + h + 1) / (N_GROUPS * SWA_HEADS)) for h in heads]
    q_cols = [new_ref[:, 0, :, h:h + 1] * (SWA_HEAD_DIM ** -0.5) for h in heads]
    k_cols = [new_ref[:, 1, :, h:h + 1] for h in heads]
    v_cols = [new_ref[:, 2, :, h:h + 1] for h in heads]
    ss, s0s = [], []
    for h in heads:
        k_h = c_ref[:, 0, h]
        co_ref[:, 0, h] = jnp.where(is_newest, k_cols[h], pltpu.roll(k_h, w - 1, axis=2))
        s = jnp.sum(k_h * q_cols[h], axis=1, keepdims=True) - slopes[h] * dist
        ss.append(jnp.where(valid, s, NEG_BIG))
        s0s.append(jnp.sum(k_cols[h] * q_cols[h], axis=1, keepdims=True))
    ms = [jnp.maximum(jnp.max(s, axis=-1, keepdims=True), s0) for s, s0 in zip(ss, s0s)]
    ps = [jnp.exp(s - m) for s, m in zip(ss, ms)]
    p0s = [jnp.exp(s0 - m) for s0, m in zip(s0s, ms)]
    ls = [jnp.sum(p, axis=-1, keepdims=True) + p0 for p, p0 in zip(ps, p0s)]
    accs = []
    for h in heads:
        v_h = c_ref[:, 1, h]
        co_ref[:, 1, h] = jnp.where(is_newest, v_cols[h], pltpu.roll(v_h, w - 1, axis=2))
        accs.append(jnp.sum(v_h * ps[h], axis=-1, keepdims=True))
    outs = [(acc + p0 * v_col) / l for acc, p0, v_col, l in zip(accs, p0s, v_cols, ls)]
    o_ref[...] = jnp.concatenate(outs, axis=-1)
    l_ref[...] = jnp.concatenate([m + jnp.log(l) for m, l in zip(ms, ls)], axis=-1)


def _shift_attend(cache_t, new_t, prev_out, *, layer, group, block_bytes=8 * 1024 * 1024):
    n_att, b = cache_t.shape[:2]
    w = cache_t.shape[-1]
    bb = _pick_tile(b, max(1, block_bytes // (2 * SWA_WIDTH * w * 4)))
    blk = (None, bb, 2, SWA_HEADS, SWA_HEAD_DIM, w)
    in_specs = [pl.BlockSpec(blk, lambda i: (layer, i, 0, 0, 0, 0)),
                pl.BlockSpec((bb, 3, SWA_HEAD_DIM, SWA_HEADS), lambda i: (i, 0, 0, 0))]
    args = [cache_t, new_t]
    aliases = {}
    if prev_out is not None:
        in_specs.append(pl.BlockSpec(memory_space=pl.ANY))
        args.append(prev_out)
        aliases = {2: 0}
    return pl.pallas_call(
        functools.partial(_shift_attend_kernel, group=group, has_prev=prev_out is not None),
        grid=(b // bb,),
        in_specs=in_specs,
        out_specs=[pl.BlockSpec(blk, lambda i: (layer, i, 0, 0, 0, 0)),
                   pl.BlockSpec((bb, SWA_HEAD_DIM, SWA_HEADS), lambda i: (i, 0, 0)),
                   pl.BlockSpec((bb, 1, SWA_HEADS), lambda i: (i, 0, 0))],
        out_shape=[jax.ShapeDtypeStruct(cache_t.shape, cache_t.dtype),
                   jax.ShapeDtypeStruct((b, SWA_HEAD_DIM, SWA_HEADS), F32),
                   jax.ShapeDtypeStruct((b, 1, SWA_HEADS), F32)],
        input_output_aliases=aliases,
        compiler_params=_cparams(("parallel",)),
        name="swa_shift_attend_g%d" % group,
    )(*args)


def _group_merge_kernel(o0, o1, o2, l0, l1, l2, y_ref):
    a0, a1, a2 = l0[...], l1[...], l2[...]
    m = jnp.maximum(jnp.maximum(a0, a1), a2)
    e0, e1, e2 = jnp.exp(a0 - m), jnp.exp(a1 - m), jnp.exp(a2 - m)
    y_ref[...] = (e0 * o0[...] + e1 * o1[...] + e2 * o2[...]) / (e0 + e1 + e2)


def _group_merge(outs, lses):
    return pl.pallas_call(
        _group_merge_kernel,
        out_shape=jax.ShapeDtypeStruct(outs[0].shape, F32),
        name="swa_group_merge",
    )(*outs, *lses)


def _rwkv_proj_kernel(*refs, seq_mode, has_vres, tiles_per_seq):
    it = iter(refs)
    x_ref = next(it)
    prev_ref = next(it)
    vfirst_ref = next(it) if has_vres else None
    g_ref, mix_ref, wr_ref, wk_ref, wv_ref = next(it), next(it), next(it), next(it), next(it)
    w0_ref, w1_ref, w2_ref = next(it), next(it), next(it)
    a0_ref, a1_ref, a2_ref = next(it), next(it), next(it)
    if has_vres:
        v0_ref, v1_ref, v2_ref = next(it), next(it), next(it)
    g1_ref, g2_ref, kk_ref, ka_ref = next(it), next(it), next(it), next(it)
    hsel_ref, hselt_ref = next(it), next(it)
    r_out, ld_out, k_out, v_out, a_out, b_out, g_out, xn_last_out = (next(it) for _ in range(8))

    xn = _rms(x_ref[...], g_ref[...])
    tm = xn.shape[0]
    if seq_mode:
        i = pl.program_id(0)
        pr = _rms(prev_ref[...], g_ref[...])[7:8]
        pr = jnp.where(i % tiles_per_seq == 0, 0.0, pr)
        row = lax.broadcasted_iota(jnp.int32, xn.shape, 0)
        xp = jnp.where(row == 0, pr, pltpu.roll(xn, 1, axis=0))
    else:
        xp = prev_ref[...]
    xx = xp - xn
    mix = mix_ref[...]
    xr, xw, xk, xv, xa, xg = (xn + xx * mix[j:j + 1] for j in range(6))
    r = _bdot(xr, wr_ref[...])
    k = _bdot(xk, wk_ref[...])
    v = _bdot(xv, wv_ref[...])
    z = w0_ref[...] + _bdot(jnp.tanh(_bdot(xw, w1_ref[...])), w2_ref[...])
    w_log = -(jnp.maximum(-z, 0.0) + jnp.log(1.0 + jnp.exp(-jnp.abs(z)))) - 0.5
    ld = -jnp.exp(w_log)
    a = _sigmoid(a0_ref[...] + _bdot(_bdot(xa, a1_ref[...]), a2_ref[...]))
    g = _bdot(_sigmoid(_bdot(xg, g1_ref[...])), g2_ref[...])
    if has_vres:
        v = v + (vfirst_ref[...] - v) * _sigmoid(v0_ref[...] + _bdot(_bdot(xv, v1_ref[...]), v2_ref[...]))
    kk = k * kk_ref[...]
    ssq = jnp.dot((kk * kk).astype(BF16), hsel_ref[...], preferred_element_type=F32)
    rs = lax.rsqrt(jnp.maximum(ssq, 1e-24))
    kk = kk * _dot_sel(rs, hselt_ref[...])
    k = k * (1.0 + (a - 1.0) * ka_ref[...])
    r_out[...] = r
    ld_out[...] = ld
    k_out[...] = k
    v_out[...] = v
    a_out[...] = -kk
    b_out[...] = kk * a
    g_out[...] = g
    xn_last_out[...] = xn[tm - 8:tm] if seq_mode else xn


def _head_selectors(d):
    heads = d // RW_HEAD
    ch = jnp.arange(d) // RW_HEAD
    hsel = (ch[:, None] == jnp.arange(128)[None, :]).astype(BF16)
    return hsel, hsel.T


def _rwkv_proj(x, prev, v_first, p, *, seq_mode, seq, tm_target=256):
    n, d = x.shape
    tm = _pick_tile(seq if seq_mode else n, tm_target)
    has_vres = v_first is not None
    full = lambda a: pl.BlockSpec(a.shape, lambda i: (0,) * a.ndim)
    tile = pl.BlockSpec((tm, d), lambda i: (i, 0))
    args, specs = [x], [tile]
    if seq_mode:
        args.append(x)
        specs.append(pl.BlockSpec((8, d), lambda i: (jnp.maximum(i * (tm // 8) - 1, 0), 0)))
    else:
        args.append(prev)
        specs.append(tile)
    if has_vres:
        args.append(v_first)
        specs.append(tile)
    hsel, hselt = _head_selectors(d)
    names = ["norm_g", "mix", "w_r", "w_k", "w_v", "w0", "w1", "w2", "a0", "a1", "a2"]
    if has_vres:
        names += ["v0", "v1", "v2"]
    names += ["g1", "g2", "k_k", "k_a"]
    consts = [p[name] for name in names] + [(hsel, full(hsel)), (hselt, full(hselt))]
    args += [c[0] for c in consts]
    specs += [c[1] for c in consts]
    outs = pl.pallas_call(
        functools.partial(_rwkv_proj_kernel, seq_mode=seq_mode, has_vres=has_vres,
                          tiles_per_seq=(seq // tm) if seq_mode else 1),
        grid=(n // tm,),
        in_specs=specs,
        out_specs=[tile] * 7 + [pl.BlockSpec((8, d), lambda i: (i, 0)) if seq_mode else tile],
        out_shape=[jax.ShapeDtypeStruct((n, d), F32)] * 7
        + [jax.ShapeDtypeStruct((n // tm * 8 if seq_mode else n, d), F32)],
        compiler_params=_cparams(("parallel",)),
        name="rwkv_proj",
    )(*args)
    return outs


def _wkv_chunk_kernel(r_ref, ld_ref, k_ref, v_ref, a_ref, b_ref, lnw_ref, lnb_ref, rk_ref,
                      y_ref, s_out_ref, s_ref, pre_ref, mat_ref, gam_ref, *, n_chunks):
    C = WKV_CHUNK
    HD = RW_HEAD
    P = 2 * RW_HEAD
    t = pl.program_id(2)
    slot_wr = t % 2
    slot_rd = 1 - slot_wr

    @pl.when(t == 0)
    def _():
        s_ref[...] = jnp.zeros_like(s_ref)
        pre_ref[1] = jnp.zeros(pre_ref.shape[1:], F32)
        mat_ref[1] = jnp.zeros(mat_ref.shape[1:], F32)
        gam_ref[1] = jnp.zeros(gam_ref.shape[1:], F32)

    ri = lax.broadcasted_iota(jnp.int32, (C, C), 0)
    ci = lax.broadcasted_iota(jnp.int32, (C, C), 1)
    tril = ri >= ci
    stril = ri > ci
    first = lambda rows: lax.broadcasted_iota(jnp.int32, (rows, P), 1) < HD
    first_h, first_c, first_2c = first(HD), first(C), first(2 * C)
    row_c = lax.broadcasted_iota(jnp.int32, (C, P), 0)
    lnw, lnb, rk = lnw_ref[...], lnb_ref[...], rk_ref[...]

    def prefix_sum_rows(x):
        shift = 1
        while shift < C:
            x = x + jnp.where(row_c >= shift, pltpu.roll(x, shift, axis=0), 0.0)
            shift *= 2
        return x

    def head_sums(x):
        lo = jnp.sum(jnp.where(first_c, x, 0.0), axis=-1, keepdims=True)
        hi = jnp.sum(jnp.where(first_c, 0.0, x), axis=-1, keepdims=True)
        return jnp.where(first_c, lo, hi)

    chunks = []
    chains = []
    for c in range(n_chunks):
        sl = slice(c * C, (c + 1) * C)
        rc, ldc, kc, vc, ac, bc = (z[sl, :] for z in (r_ref, ld_ref, k_ref, v_ref, a_ref, b_ref))
        cum = prefix_sum_rows(ldc)
        e_pos = jnp.exp(cum)
        e_neg = jnp.exp(-cum)
        at = ac * jnp.exp(cum - ldc)
        rt = rc * e_pos
        bt = bc * e_neg
        kt = kc * e_neg
        last = cum[C - 1:C]
        e_last = jnp.exp(last - cum)
        bh = bc * e_last
        kh = kc * e_last
        gamma = jnp.exp(last)
        pre_ref[slot_wr, c, 2] = head_sums(rc * kc * rk) * vc
        gam_ref[slot_wr, c] = jnp.broadcast_to(gamma, (8, P))
        chunk = dict(at_swapped=pltpu.roll(at, HD, axis=1), rt=rt, v=vc,
                     ar=jnp.concatenate([at, rt], axis=0), bk=jnp.concatenate([bt, kt], axis=0),
                     bkh=jnp.concatenate([bh, kh], axis=0))
        chunks.append(chunk)
        chains += [dict(h=0, chunk=chunk), dict(h=1, chunk=chunk)]

    def stage_scores():
        for ch in chains:
            ck = ch["chunk"]
            mine = first_2c if ch["h"] == 0 else ~first_2c
            sc = _bdot_nt(jnp.where(mine, ck["ar"], 0.0), ck["bk"])
            ch["lp"] = jnp.where(stril, sc[:C, :C], 0.0)
            ch["ak"] = jnp.where(stril, sc[:C, C:], 0.0)
            ch["rbk"] = jnp.concatenate([jnp.where(tril, sc[C:, :C], 0.0),
                                         jnp.where(tril, sc[C:, C:], 0.0)], axis=1)

    def stage_x0():
        for ch in chains:
            ck = ch["chunk"]
            akv = _bdot(ch["ak"], ck["v"])
            ch["x"] = (jnp.where(first_c, akv, ck["at_swapped"]) if ch["h"] == 0
                       else jnp.where(first_c, ck["at_swapped"], akv))

    def stage_level(square):
        nxt = [ch["x"] + _bdot(ch["lp"], ch["x"]) for ch in chains]
        if square:
            for ch in chains:
                ch["lp"] = _bdot(ch["lp"], ch["lp"])
        for ch, x in zip(chains, nxt):
            ch["x"] = x

    def stage_final_dots():
        for ch in chains:
            ck = ch["chunk"]
            vz = jnp.where(first_c, ck["v"], 0.0) if ch["h"] == 0 else jnp.where(first_c, 0.0, ck["v"])
            z = jnp.concatenate([ch["x"], vz], axis=0)
            ch["ry"] = _bdot(ch["rbk"], z)
            ch["qh"] = _bdot_tn(z, ck["bkh"])

    def stage_final_stores():
        for c, ck in enumerate(chunks):
            ry_a, ry_b = chains[2 * c]["ry"], chains[2 * c + 1]["ry"]
            qa, qb = chains[2 * c]["qh"], chains[2 * c + 1]["qh"]
            pre_ref[slot_wr, c, 0] = ck["rt"] + pltpu.roll(jnp.where(first_c, ry_b, ry_a), HD, axis=1)
            pre_ref[slot_wr, c, 1] = jnp.where(first_c, ry_a, ry_b)
            mat_ref[slot_wr, c, 0] = jnp.concatenate([jnp.where(first_h, qa[HD:], 0.0),
                                                      jnp.where(first_h, 0.0, qb[:HD])], axis=0)
            mat_ref[slot_wr, c, 1] = jnp.concatenate([jnp.where(first_h, qa[:HD], 0.0),
                                                      jnp.where(first_h, 0.0, qb[HD:])], axis=0)

    state = [s_ref[...]]

    def state_step(c):
        s0 = state[0]
        y = _bdot_nt(pre_ref[slot_rd, c, 0], s0) + pre_ref[slot_rd, c, 1]
        state[0] = s0 * gam_ref[slot_rd, c, 0:1] + _bdot(s0, mat_ref[slot_rd, c, 0]) + mat_ref[slot_rd, c, 1]
        mu = head_sums(y) * (1.0 / HD)
        yc = y - mu
        var = head_sums(yc * yc) * (1.0 / HD)
        y_ref[c * C:(c + 1) * C, :] = yc * lax.rsqrt(var + RW_LN_EPS) * lnw + lnb + pre_ref[slot_rd, c, 2]

    stages = [stage_scores, stage_x0]
    n = 1
    while n < C:
        stages.append(functools.partial(stage_level, 2 * n < C))
        n *= 2
    stages += [stage_final_dots, stage_final_stores]
    for si, stage in enumerate(stages):
        stage()
        for c in range(si * n_chunks // len(stages), (si + 1) * n_chunks // len(stages)):
            state_step(c)
    s_ref[...] = state[0]

    @pl.when(t == pl.num_programs(2) - 1)
    def _():
        s_out_ref[0] = s_ref[0:HD, 0:HD]
        s_out_ref[1] = s_ref[HD:P, HD:P]


def _wkv_chunked(r, ld, k, v, a, b, ln_w, ln_b, r_k, layer, *, bsz, seq, tb_target=512):
    n, d = r.shape
    heads = d // RW_HEAD
    pairs = heads // 2
    tb = _pick_tile(seq, tb_target)
    per_seq = seq // tb
    n_chunks = tb // WKV_CHUNK
    in_tile = pl.BlockSpec((tb, 128), lambda bi, hp, t: (bi * per_seq + jnp.minimum(t, per_seq - 1), hp))
    out_tile = pl.BlockSpec((tb, 128), lambda bi, hp, t: (bi * per_seq + jnp.maximum(t - 1, 0), hp))
    vec = pl.BlockSpec((None, 1, 128), lambda bi, hp, t: (layer, 0, hp))
    y, st = pl.pallas_call(
        functools.partial(_wkv_chunk_kernel, n_chunks=n_chunks),
        grid=(bsz, pairs, per_seq + 1),
        in_specs=[in_tile] * 6 + [vec] * 3,
        out_specs=[out_tile, pl.BlockSpec((None, 2, RW_HEAD, RW_HEAD), lambda bi, hp, t: (bi, hp, 0, 0))],
        out_shape=[jax.ShapeDtypeStruct((n, d), F32),
                   jax.ShapeDtypeStruct((bsz, heads, RW_HEAD, RW_HEAD), F32)],
        scratch_shapes=[pltpu.VMEM((128, 128), F32),
                        pltpu.VMEM((2, n_chunks, 3, WKV_CHUNK, 128), F32),
                        pltpu.VMEM((2, n_chunks, 2, 128, 128), F32),
                        pltpu.VMEM((2, n_chunks, 8, 128), F32)],
        compiler_params=_cparams(("parallel", "parallel", "arbitrary")),
        name="wkv_chunked",
    )(r, ld, k, v, a, b, ln_w, ln_b, r_k)
    return y, st


def _wkv_step_kernel(*refs, has_prev):
    it = iter(refs)
    s_ref, r_ref, ld_ref, k_ref, a_ref, b_ref, rk_ref, v_ref, lnw_ref, lnb_ref = (next(it) for _ in range(10))
    if has_prev:
        next(it)
    s_out, y_out = next(it), next(it)
    s = s_ref[...]
    r, k = r_ref[...], k_ref[...]
    v = v_ref[...]
    sa = jnp.sum(s * a_ref[...][:, None], axis=2, keepdims=True)
    s_new = s * jnp.exp(ld_ref[...])[:, None] + sa * b_ref[...][:, None] + v * k[:, None]
    y = jnp.sum(s_new * r[:, None], axis=2, keepdims=True)
    mu = jnp.mean(y, axis=1, keepdims=True)
    yc = y - mu
    var = jnp.mean(yc * yc, axis=1, keepdims=True)
    yn = yc * lax.rsqrt(var + RW_LN_EPS) * lnw_ref[...] + lnb_ref[...]
    bonus = jnp.sum(r * k * rk_ref[...], axis=1, keepdims=True)[:, None] * v
    s_out[...] = s_new
    y_out[...] = yn + bonus


def _wkv_step(state_t, prev_out, layer, r, ld, k, v, a, b, ln_w, ln_b, r_k, *, hb=2):
    heads, bsz = state_t.shape[1], state_t.shape[-1]
    d = heads * RW_HEAD
    per_key = lambda z: z.T.reshape(heads, RW_HEAD, bsz)
    per_val = lambda z: z.T.reshape(heads, RW_HEAD, 1, bsz)
    bcast = lambda p, shape: jnp.broadcast_to(p.reshape(shape[:-1] + (1,)), shape)
    st_spec = pl.BlockSpec((None, hb, RW_HEAD, RW_HEAD, bsz), lambda i: (layer, i, 0, 0, 0))
    key_spec = pl.BlockSpec((hb, RW_HEAD, bsz), lambda i: (i, 0, 0))
    val_spec = pl.BlockSpec((hb, RW_HEAD, 1, bsz), lambda i: (i, 0, 0, 0))
    in_specs = [st_spec] + [key_spec] * 6 + [val_spec] * 3
    args = [state_t, per_key(r), per_key(ld), per_key(k), per_key(a), per_key(b),
            bcast(r_k, (heads, RW_HEAD, bsz)), per_val(v),
            bcast(ln_w, (heads, RW_HEAD, 1, bsz)), bcast(ln_b, (heads, RW_HEAD, 1, bsz))]
    aliases = {}
    if prev_out is not None:
        in_specs.append(pl.BlockSpec(memory_space=pl.ANY))
        args.append(prev_out)
        aliases = {10: 0}
    s_new, y = pl.pallas_call(
        functools.partial(_wkv_step_kernel, has_prev=prev_out is not None),
        grid=(heads // hb,),
        in_specs=in_specs,
        out_specs=[st_spec, val_spec],
        out_shape=[jax.ShapeDtypeStruct(state_t.shape, F32),
                   jax.ShapeDtypeStruct((heads, RW_HEAD, 1, bsz), F32)],
        input_output_aliases=aliases,
        compiler_params=_cparams(("parallel",)),
        name="wkv_step",
    )(*args)
    return y.reshape(d, bsz).T, s_new


def _run_layers(x, mem_kv, wkv0, shift0, swa_bufs, W, *, bsz, seq):
    is_prompt = swa_bufs is None
    depth = W["norm_g"].shape[0]
    new_wkv, new_shift = [], []
    new_kv = [[] for _ in range(N_GROUPS)]
    v_first = None
    for l in range(depth):
        ng = lambda j: _param(W["norm_g"], l, j)
        x = _ffn(x, ng(0), W["ffn_w_in"], W["ffn_w_out"], l, 0)
        i = l // 2
        if l % 2 == 0:
            at_i = lambda name, *more: _param(W[name], i, *more)
            p = dict(norm_g=ng(1), mix=at_i("rw_mix"), w_r=at_i("rw_w_rkv", 0), w_k=at_i("rw_w_rkv", 1),
                     w_v=at_i("rw_w_rkv", 2), w0=at_i("rw_w0"), w1=at_i("rw_w1"), w2=at_i("rw_w2"),
                     a0=at_i("rw_a0"), a1=at_i("rw_a1"), a2=at_i("rw_a2"), g1=at_i("rw_g1"), g2=at_i("rw_g2"),
                     k_k=at_i("rw_k_k"), k_a=at_i("rw_k_a"))
            if i > 0:
                p.update(v0=_param(W["rw_v0"], i - 1), v1=_param(W["rw_v1"], i - 1), v2=_param(W["rw_v2"], i - 1))
            prev = None if is_prompt else shift0[i]
            r, ld, k, v, a, b, g, xn_last = _rwkv_proj(x, prev, v_first if i > 0 else None, p,
                                                      seq_mode=is_prompt, seq=seq)
            if i == 0:
                v_first = v
            if is_prompt:
                y, s_new = _wkv_chunked(r, ld, k, v, a, b, W["rw_ln_w"], W["rw_ln_b"], W["rw_r_k"], i,
                                        bsz=bsz, seq=seq)
                tiles = xn_last.shape[0] // 8 // bsz
                shift_new = xn_last.reshape(bsz, tiles, 8, -1)[:, -1, -1]
                new_wkv.append(s_new)
            else:
                y, s_new = _wkv_step(wkv0, new_wkv[0] if new_wkv else None, i, r, ld, k, v, a, b,
                                     W["rw_ln_w"][i, 0], W["rw_ln_b"][i, 0], W["rw_r_k"][i, 0])
                shift_new = xn_last
                new_wkv = [s_new]
            new_shift.append(shift_new)
            x = _linear(y, W["rw_w_o"], i, mul=g, res=x, name="rwkv_out")
        else:
            outs, lses = [], []
            if is_prompt:
                qkv_groups = _swa_qkv_prompt(x, ng(1), W["swa_w_qkv"], i, bsz=bsz, seq=seq)
                for gi in range(N_GROUPS):
                    o, lse = _band_attention(qkv_groups[gi], gi)
                    outs.append(o)
                    lses.append(lse)
                    dil = SWA_DILATIONS[gi]
                    keep = min(SWA_WINDOWS[gi], seq)
                    kv_tail = qkv_groups[gi][:, :, (seq - keep) // dil:, SWA_WIDTH:]
                    new_kv[gi].append(jnp.swapaxes(kv_tail, 1, 2).reshape(
                        bsz, keep, 2, SWA_HEADS, SWA_HEAD_DIM))
                x = _swa_merge(outs, lses, x, _param(W["swa_w_o"], i), seq=seq)
            else:
                qkv = _linear(x, W["swa_w_qkv"], i, norm_g=ng(1), tn_target=512, name="swa_qkv")
                qkv5 = qkv.reshape(bsz, N_GROUPS, 3, SWA_HEADS, SWA_HEAD_DIM)
                for gi in range(N_GROUPS):
                    prev_out = new_kv[gi][0] if new_kv[gi] else None
                    new_t = jnp.swapaxes(qkv5[:, gi], -1, -2)
                    cache_out, o, lse = _shift_attend(swa_bufs[gi], new_t, prev_out, layer=i, group=gi)
                    new_kv[gi] = [cache_out]
                    outs.append(o)
                    lses.append(lse)
                y = jnp.swapaxes(_group_merge(outs, lses), 1, 2).reshape(bsz, SWA_WIDTH)
                x = _linear(y, W["swa_w_o"], i, res=x, name="swa_out")
        if is_prompt:
            x = _xa_prompt(x, ng(2), _param(W["xa_w_q"], l), mem_kv, _param(W["xa_w_o"], l), l, seq=seq)
        else:
            q = _linear(x, W["xa_w_q"], l, norm_g=ng(2), name="xa_q")
            o = _xa_sample(q, mem_kv, l)
            x = _linear(o, W["xa_w_o"], l, res=x, name="xa_out")
        x = _ffn(x, ng(3), W["ffn_w_in"], W["ffn_w_out"], l, 1,
                 final_g=_param(W["final_norm_g"], 0) if l == depth - 1 else None)
    return x, new_wkv, new_shift, new_kv


def kernel(x_prompt, x_sample, cache_mem_kv, state_rwkv_wkv, state_rwkv_shift, cache_swa_kv_g0, cache_swa_kv_g1, cache_swa_kv_g2, mem_prompt, norm_g, mem_norm_g, final_norm_g, ffn_w_in, ffn_w_out, rw_mix, rw_w_rkv, rw_w_o, rw_w0, rw_w1, rw_w2, rw_a0, rw_a1, rw_a2, rw_v0, rw_v1, rw_v2, rw_g1, rw_g2, rw_k_k, rw_k_a, rw_r_k, rw_ln_w, rw_ln_b, swa_w_qkv, swa_w_o, xa_w_q, xa_w_kv, xa_w_o):
    bf = lambda w: w.astype(BF16)
    vec = lambda a: a.reshape(a.shape[:-1] + (1, a.shape[-1]))
    depth = norm_g.shape[0]
    bsz, seq, d = x_prompt.shape
    n_mem = mem_prompt.shape[1]
    W = dict(norm_g=vec(norm_g), final_norm_g=final_norm_g.reshape(1, 1, d),
             ffn_w_in=bf(ffn_w_in), ffn_w_out=bf(ffn_w_out),
             rw_mix=rw_mix, rw_w_rkv=bf(rw_w_rkv), rw_w_o=bf(rw_w_o), rw_w0=vec(rw_w0), rw_w1=bf(rw_w1),
             rw_w2=bf(rw_w2), rw_a0=vec(rw_a0), rw_a1=bf(rw_a1), rw_a2=bf(rw_a2), rw_v0=vec(rw_v0),
             rw_v1=bf(rw_v1), rw_v2=bf(rw_v2), rw_g1=bf(rw_g1), rw_g2=bf(rw_g2), rw_k_k=vec(rw_k_k),
             rw_k_a=vec(rw_k_a), rw_r_k=rw_r_k.reshape(-1, 1, d), rw_ln_w=vec(rw_ln_w), rw_ln_b=vec(rw_ln_b),
             swa_w_qkv=bf(swa_w_qkv), swa_w_o=bf(swa_w_o), xa_w_q=bf(xa_w_q), xa_w_o=bf(xa_w_o))

    mem2 = mem_prompt.reshape(bsz * n_mem, d)
    xa_w_kv_b = bf(xa_w_kv)
    mem_g = vec(mem_norm_g)
    mem_kv_p = jnp.stack([_linear(mem2, xa_w_kv_b, l, norm_g=_param(mem_g, l), name="mem_kv")
                          for l in range(depth)])
    y_p, wkv_p, shift_p, kv_p = _run_layers(
        x_prompt.reshape(bsz * seq, d), mem_kv_p.reshape(depth, bsz, n_mem, 2 * XA_WIDTH),
        None, None, None, W, bsz=bsz, seq=seq)

    dbsz, dseq, _ = x_sample.shape
    assert dseq == 1
    caches = tuple(jnp.transpose(c, (0, 1, 3, 4, 5, 2))
                   for c in (cache_swa_kv_g0, cache_swa_kv_g1, cache_swa_kv_g2))
    wkv_t = jnp.transpose(state_rwkv_wkv, (0, 2, 3, 4, 1))
    y_s, wkv_s, shift_s, kv_s = _run_layers(
        x_sample.reshape(dbsz, d), cache_mem_kv,
        wkv_t, state_rwkv_shift, caches, W, bsz=dbsz, seq=1)
    swa_s = tuple(jnp.transpose(kv_s[g][0], (0, 1, 5, 2, 3, 4)) for g in range(N_GROUPS))
    wkv_s = jnp.transpose(wkv_s[0], (0, 4, 1, 2, 3))

    return (y_p.reshape(bsz, seq, d), y_s.reshape(dbsz, 1, d),
            mem_kv_p.reshape(depth, bsz, n_mem, 2, XA_HEADS, XA_HEAD_DIM),
            jnp.stack(wkv_p), jnp.stack(shift_p),
            jnp.stack(kv_p[0]), jnp.stack(kv_p[1]), jnp.stack(kv_p[2]),
            wkv_s, jnp.stack(shift_s), swa_s[0], swa_s[1], swa_s[2])
```

```python
import functools

import jax
import jax.numpy as jnp
from jax import lax
from jax.experimental import pallas as pl
from jax.experimental.pallas import tpu as pltpu

F32 = jnp.float32
BF16 = jnp.bfloat16

NORM_EPS = 1e-6
RW_HEAD = 64
RW_LN_EPS = RW_HEAD * 1e-5
SWA_WINDOWS = (128, 512, 2048)
SWA_DILATIONS = (1, 4, 16)
N_GROUPS = 3
SWA_HEADS = 8
SWA_HEAD_DIM = 64
SWA_WIDTH = SWA_HEADS * SWA_HEAD_DIM
SWA_BLOCK = 128
XA_HEADS = 4
XA_HEAD_DIM = 128
XA_WIDTH = XA_HEADS * XA_HEAD_DIM
WKV_CHUNK = 64
NEG_BIG = -1e30
VMEM_LIMIT = 56 * 1024 * 1024


def _cparams(sem):
    return pltpu.CompilerParams(dimension_semantics=sem, vmem_limit_bytes=VMEM_LIMIT)


def _bdot(a, b):
    return jnp.dot(a.astype(BF16), b.astype(BF16), preferred_element_type=F32)


def _bdot_nt(a, b):
    return lax.dot_general(a.astype(BF16), b.astype(BF16), (((1,), (1,)), ((), ())),
                           preferred_element_type=F32)


def _bdot_tn(a, b):
    return lax.dot_general(a.astype(BF16), b.astype(BF16), (((0,), (0,)), ((), ())),
                           preferred_element_type=F32)


def _split3(x):
    h1 = x.astype(BF16)
    r1 = x - h1.astype(F32)
    h2 = r1.astype(BF16)
    h3 = (r1 - h2.astype(F32)).astype(BF16)
    return h1, h2, h3


def _dot_sel(x, sel_bf16):
    h1, h2, h3 = _split3(x)
    d = lambda h: jnp.dot(h, sel_bf16, preferred_element_type=F32)
    return d(h1) + d(h2) + d(h3)


def _rms(x, g):
    return x * lax.rsqrt(jnp.mean(x * x, axis=-1, keepdims=True) + NORM_EPS) * g


def _sigmoid(x):
    return 1.0 / (1.0 + jnp.exp(-x))


def _param(arr, *idx):
    blk = (None,) * len(idx) + tuple(arr.shape[len(idx):])
    tail = (0,) * (arr.ndim - len(idx))
    return arr, pl.BlockSpec(blk, lambda *_: tuple(idx) + tail)


def _pick_tile(n, target):
    t = min(n, target)
    while n % t:
        t //= 2
    return t


def _ffn_kernel(*refs, final_norm):
    if final_norm:
        x_ref, g_ref, wg_ref, wu_ref, wo_ref, fg_ref, o_ref, xn_ref, acc_ref = refs
    else:
        x_ref, g_ref, wg_ref, wu_ref, wo_ref, o_ref, xn_ref, acc_ref = refs
    j = pl.program_id(1)

    @pl.when(j == 0)
    def _():
        xn_ref[...] = _rms(x_ref[...], g_ref[...]).astype(BF16)
        acc_ref[...] = jnp.zeros_like(acc_ref)

    xn = xn_ref[...]
    gate = jnp.dot(xn, wg_ref[...], preferred_element_type=F32)
    up = jnp.dot(xn, wu_ref[...], preferred_element_type=F32)
    h = (gate * _sigmoid(gate) * up).astype(BF16)
    acc_ref[...] += jnp.dot(h, wo_ref[...], preferred_element_type=F32)

    @pl.when(j == pl.num_programs(1) - 1)
    def _():
        y = x_ref[...] + 0.5 * acc_ref[...]
        o_ref[...] = _rms(y, fg_ref[...]) if final_norm else y


def _ffn(x, g, w_in, w_out, layer, which, *, final_g=None, tm_target=512, tf_target=1408):
    n, d = x.shape
    d_ff = w_out.shape[2]
    tm = _pick_tile(n, tm_target)
    tf = tf_target if d_ff % tf_target == 0 else 128
    nf = d_ff // tf
    in_specs = [
        pl.BlockSpec((tm, d), lambda i, j: (i, 0)),
        g[1],
        pl.BlockSpec((None, None, d, tf), lambda i, j: (layer, which, 0, j)),
        pl.BlockSpec((None, None, d, tf), lambda i, j: (layer, which, 0, j + nf)),
        pl.BlockSpec((None, None, tf, d), lambda i, j: (layer, which, j, 0)),
    ]
    args = [x, g[0], w_in, w_in, w_out]
    if final_g is not None:
        in_specs.append(final_g[1])
        args.append(final_g[0])
    return pl.pallas_call(
        functools.partial(_ffn_kernel, final_norm=final_g is not None),
        grid=(n // tm, nf),
        in_specs=in_specs,
        out_specs=pl.BlockSpec((tm, d), lambda i, j: (i, 0)),
        out_shape=jax.ShapeDtypeStruct((n, d), F32),
        scratch_shapes=[pltpu.VMEM((tm, d), BF16), pltpu.VMEM((tm, d), F32)],
        compiler_params=_cparams(("parallel", "arbitrary")),
        name="ffn",
    )(*args)


def _linear_kernel(*refs, mode, has_res):
    it = iter(refs)
    x_ref = next(it)
    p_ref = next(it) if mode in ("norm", "mul") else None
    w_ref = next(it)
    res_ref = next(it) if has_res else None
    o_ref = next(it)
    xs_ref = next(it)

    @pl.when(pl.program_id(1) == 0)
    def _():
        x = x_ref[...]
        if mode == "norm":
            x = _rms(x, p_ref[...])
        elif mode == "mul":
            x = x * p_ref[...]
        xs_ref[...] = x.astype(BF16)

    o = jnp.dot(xs_ref[...], w_ref[...], preferred_element_type=F32)
    if has_res:
        o = res_ref[...] + o
    o_ref[...] = o


def _linear(x, w, layer, *, norm_g=None, mul=None, res=None, tm_target=512, tn_target=1024, name="linear"):
    n, kd = x.shape
    nout = w.shape[2]
    tm = _pick_tile(n, tm_target)
    tn = _pick_tile(nout, tn_target)
    mode = "norm" if norm_g is not None else ("mul" if mul is not None else "plain")
    args = [x]
    specs = [pl.BlockSpec((tm, kd), lambda i, j: (i, 0))]
    if mode == "norm":
        args.append(norm_g[0])
        specs.append(norm_g[1])
    elif mode == "mul":
        args.append(mul)
        specs.append(pl.BlockSpec((tm, kd), lambda i, j: (i, 0)))
    args.append(w)
    specs.append(pl.BlockSpec((None, kd, tn), lambda i, j: (layer, 0, j)))
    if res is not None:
        args.append(res)
        specs.append(pl.BlockSpec((tm, tn), lambda i, j: (i, j)))
    return pl.pallas_call(
        functools.partial(_linear_kernel, mode=mode, has_res=res is not None),
        grid=(n // tm, nout // tn),
        in_specs=specs,
        out_specs=pl.BlockSpec((tm, tn), lambda i, j: (i, j)),
        out_shape=jax.ShapeDtypeStruct((n, nout), F32),
        scratch_shapes=[pltpu.VMEM((tm, kd), BF16)],
        compiler_params=_cparams(("parallel", "arbitrary")),
        name=name,
    )(*args)


def _xa_prompt_kernel(x_ref, g_ref, wq_ref, kv_ref, wo_ref, o_ref):
    x = x_ref[...]
    q = jnp.dot(_rms(x, g_ref[...]).astype(BF16), wq_ref[...], preferred_element_type=F32)
    kv = kv_ref[...].astype(BF16)
    outs = []
    for h in range(XA_HEADS):
        lo = h * XA_HEAD_DIM
        s = _bdot_nt(q[:, lo:lo + XA_HEAD_DIM], kv[:, lo:lo + XA_HEAD_DIM]) * (XA_HEAD_DIM ** -0.5)
        m = jnp.max(s, axis=-1, keepdims=True)
        p = jnp.exp(s - m)
        l = jnp.sum(p, axis=-1, keepdims=True)
        o = jnp.dot(p.astype(BF16), kv[:, XA_WIDTH + lo:XA_WIDTH + lo + XA_HEAD_DIM],
                    preferred_element_type=F32)
        outs.append(o / l)
    o = jnp.concatenate(outs, axis=-1)
    o_ref[...] = x + jnp.dot(o.astype(BF16), wo_ref[...], preferred_element_type=F32)


def _xa_prompt(x, g, wq, kv, wo, layer, *, seq, tm_target=512):
    n, d = x.shape
    n_mem = kv.shape[2]
    tm = _pick_tile(seq, tm_target)
    per_b = seq // tm
    return pl.pallas_call(
        _xa_prompt_kernel,
        grid=(n // tm,),
        in_specs=[
            pl.BlockSpec((tm, d), lambda i: (i, 0)),
            g[1],
            wq[1],
            pl.BlockSpec((None, None, n_mem, 2 * XA_WIDTH), lambda i: (layer, i // per_b, 0, 0)),
            wo[1],
        ],
        out_specs=pl.BlockSpec((tm, d), lambda i: (i, 0)),
        out_shape=jax.ShapeDtypeStruct((n, d), F32),
        compiler_params=_cparams(("parallel",)),
        name="xa_prompt",
    )(x, g[0], wq[0], kv, wo[0])


def _xa_sample_kernel(q_ref, kv_ref, o_ref):
    q = q_ref[...]
    k = kv_ref[:, :, 0]
    v = kv_ref[:, :, 1]
    s = jnp.sum(k * q[:, None], axis=-1, keepdims=True) * (XA_HEAD_DIM ** -0.5)
    m = jnp.max(s, axis=1, keepdims=True)
    p = jnp.exp(s - m)
    l = jnp.sum(p, axis=1)
    o_ref[...] = jnp.sum(p * v, axis=1) / l


def _xa_sample(q, kv, layer, *, bb=8):
    b = q.shape[0]
    n_mem = kv.shape[2]
    bb = _pick_tile(b, bb)
    hd = (XA_HEADS, XA_HEAD_DIM)
    out = pl.pallas_call(
        _xa_sample_kernel,
        grid=(b // bb,),
        in_specs=[
            pl.BlockSpec((bb,) + hd, lambda i: (i, 0, 0)),
            pl.BlockSpec((None, bb, n_mem, 2) + hd, lambda i: (layer, i, 0, 0, 0, 0)),
        ],
        out_specs=pl.BlockSpec((bb,) + hd, lambda i: (i, 0, 0)),
        out_shape=jax.ShapeDtypeStruct((b,) + hd, F32),
        compiler_params=_cparams(("parallel",)),
        name="xa_sample",
    )(q.reshape((b,) + hd), kv)
    return out.reshape(b, XA_WIDTH)


def _band_bias(group):
    dil = SWA_DILATIONS[group]
    units = SWA_WINDOWS[group] // dil
    qi = jnp.arange(SWA_BLOCK)[:, None]
    ki = jnp.arange(2 * SWA_BLOCK)[None, :]
    delta = qi + SWA_BLOCK - ki
    band = (delta >= 0) & (delta <= units)
    heads = jnp.arange(SWA_HEADS, dtype=F32)
    slopes = jnp.exp2(-8.0 * (group * SWA_HEADS + heads + 1.0) / (N_GROUPS * SWA_HEADS))
    alibi = -slopes[:, None, None] * (dil * delta).astype(F32)[None]
    later = jnp.where(band[None], alibi, NEG_BIG)
    first = jnp.where((band & (ki >= SWA_BLOCK))[None], alibi, NEG_BIG)
    return jnp.stack([first, later])


def _band_kernel(q_ref, kp_ref, kc_ref, vp_ref, vc_ref, bias_ref, o_ref, l_ref):
    n = pl.program_id(2)
    blocks = q_ref.shape[0] // SWA_BLOCK
    q = (q_ref[...] * (SWA_HEAD_DIM ** -0.5)).astype(BF16)
    k = jnp.concatenate([kp_ref[...], kc_ref[...]], axis=0).astype(BF16)
    v = jnp.concatenate([vp_ref[...], vc_ref[...]], axis=0).astype(BF16)
    first_of_seq = jnp.minimum(n, 1)
    work = [(sb, h) for sb in range(blocks) for h in range(SWA_HEADS)]
    rows = lambda sb: slice(sb * SWA_BLOCK, (sb + 1) * SWA_BLOCK)
    keys = lambda sb: slice(sb * SWA_BLOCK, (sb + 2) * SWA_BLOCK)
    lanes = lambda h: slice(h * SWA_HEAD_DIM, (h + 1) * SWA_HEAD_DIM)
    ss = [_bdot_nt(q[rows(sb), lanes(h)], k[keys(sb), lanes(h)])
          + (bias_ref[first_of_seq, h] if sb == 0 else bias_ref[1, h]) for sb, h in work]
    ms = [jnp.max(s, axis=-1, keepdims=True) for s in ss]
    ps = [jnp.exp(s - m) for s, m in zip(ss, ms)]
    ls = [jnp.sum(p, axis=-1, keepdims=True) for p in ps]
    os = [jnp.dot(p.astype(BF16), v[keys(sb), lanes(h)], preferred_element_type=F32)
          for p, (sb, h) in zip(ps, work)]
    for (sb, h), o, m, l in zip(work, os, ms, ls):
        o_ref[rows(sb), lanes(h)] = o / l
        l_ref[rows(sb), lanes(h)] = jnp.broadcast_to(m + jnp.log(l), (SWA_BLOCK, SWA_HEAD_DIM))


def _swa_qkv_kernel(x_ref, g_ref, w_ref, o0_ref, o1_ref, o2_ref, tmp_ref):
    xs = _rms(x_ref[...], g_ref[...]).astype(BF16)
    tm = xs.shape[0]
    gw = 3 * SWA_WIDTH
    for g, o_ref in enumerate((o0_ref, o1_ref, o2_ref)):
        dil = SWA_DILATIONS[g]
        out = jnp.dot(xs, w_ref[:, g * gw:(g + 1) * gw], preferred_element_type=F32)
        if dil == 1:
            o_ref[0] = out
        else:
            for cb in range(gw // 128):
                tmp_ref[cb] = out[:, cb * 128:(cb + 1) * 128]
            for r in range(dil):
                for cb in range(gw // 128):
                    o_ref[r, :, cb * 128:(cb + 1) * 128] = tmp_ref[cb, pl.ds(r, tm // dil, stride=dil), :]


def _swa_qkv_prompt(x, g, w, layer, *, bsz, seq, tm_target=512):
    n, d = x.shape
    tm = _pick_tile(seq, tm_target)
    per_b = seq // tm
    gw = 3 * SWA_WIDTH
    out_specs, out_shapes = [], []
    for dil in SWA_DILATIONS:
        out_specs.append(pl.BlockSpec((None, dil, tm // dil, gw), lambda i: (i // per_b, 0, i % per_b, 0)))
        out_shapes.append(jax.ShapeDtypeStruct((bsz, dil, seq // dil, gw), F32))
    return pl.pallas_call(
        _swa_qkv_kernel,
        grid=(n // tm,),
        in_specs=[pl.BlockSpec((tm, d), lambda i: (i, 0)),
                  g[1],
                  pl.BlockSpec((None,) + w.shape[1:], lambda i: (layer, 0, 0),
                               pipeline_mode=pl.Buffered(1))],
        out_specs=out_specs,
        out_shape=out_shapes,
        scratch_shapes=[pltpu.VMEM((gw // 128, tm, 128), F32)],
        compiler_params=_cparams(("parallel",)),
        name="swa_qkv_prompt",
    )(x, g[0], w)


def _band_attention(qkv_g, group):
    bsz, dil, l_sub, _ = qkv_g.shape
    per_step = next(c for c in (4, 2, 1) if l_sub % (c * SWA_BLOCK) == 0)
    rows = per_step * SWA_BLOCK
    blk = (None, None, rows, SWA_WIDTH)
    cur = lambda col: pl.BlockSpec(blk, lambda b, r, n: (b, r, n, col))
    prev = lambda col: pl.BlockSpec((None, None, SWA_BLOCK, SWA_WIDTH),
                                    lambda b, r, n: (b, r, jnp.maximum(n * per_step - 1, 0), col))
    out_spec = pl.BlockSpec(blk, lambda b, r, n: (b, r, n, 0))
    bias_spec = pl.BlockSpec((2, SWA_HEADS, SWA_BLOCK, 2 * SWA_BLOCK), lambda b, r, n: (0, 0, 0, 0))
    return pl.pallas_call(
        _band_kernel,
        grid=(bsz, dil, l_sub // rows),
        in_specs=[cur(0), prev(1), cur(1), prev(2), cur(2), bias_spec],
        out_specs=[out_spec, out_spec],
        out_shape=[jax.ShapeDtypeStruct((bsz, dil, l_sub, SWA_WIDTH), F32)] * 2,
        compiler_params=_cparams(("parallel", "parallel", "arbitrary")),
        name="swa_band_g%d" % group,
    )(qkv_g, qkv_g, qkv_g, qkv_g, qkv_g, _band_bias(group))


def _merge_kernel(o0, o1, o2, l0, l1, l2, x_ref, w_ref, out_ref, *tmp_refs):
    tm = x_ref.shape[0]
    vals = [o0[0], l0[0]]
    tmp = iter(tmp_refs)
    for g, (o_ref, l_ref) in ((1, (o1, l1)), (2, (o2, l2))):
        dil = SWA_DILATIONS[g]
        for ref in (o_ref, l_ref):
            t_ref = next(tmp)
            for r in range(dil):
                for cb in range(SWA_WIDTH // 128):
                    t_ref[cb, pl.ds(r, tm // dil, stride=dil), :] = ref[r, :, cb * 128:(cb + 1) * 128]
            vals.append(jnp.concatenate([t_ref[cb] for cb in range(SWA_WIDTH // 128)], axis=-1))
    o_0, a0, o_1, a1, o_2, a2 = vals
    m = jnp.maximum(jnp.maximum(a0, a1), a2)
    e0, e1, e2 = jnp.exp(a0 - m), jnp.exp(a1 - m), jnp.exp(a2 - m)
    y = (e0 * o_0 + e1 * o_1 + e2 * o_2) / (e0 + e1 + e2)
    out_ref[...] = x_ref[...] + jnp.dot(y.astype(BF16), w_ref[...], preferred_element_type=F32)


def _swa_merge(outs, lses, x, wo, *, seq, tm_target=256):
    n, d = x.shape
    tm = _pick_tile(seq, tm_target)
    per_b = seq // tm
    specs = [pl.BlockSpec((None, dil, tm // dil, SWA_WIDTH), lambda i: (i // per_b, 0, i % per_b, 0))
             for dil in SWA_DILATIONS]
    return pl.pallas_call(
        _merge_kernel,
        grid=(n // tm,),
        in_specs=specs + specs + [pl.BlockSpec((tm, d), lambda i: (i, 0)),
                                  wo[1]],
        out_specs=pl.BlockSpec((tm, d), lambda i: (i, 0)),
        out_shape=jax.ShapeDtypeStruct((n, d), F32),
        scratch_shapes=[pltpu.VMEM((SWA_WIDTH // 128, tm, 128), F32)] * 4,
        compiler_params=_cparams(("parallel",)),
        name="swa_merge",
    )(*outs, *lses, x, wo[0])


def _shift_attend_kernel(*refs, group, has_prev):
    it = iter(refs)
    c_ref, new_ref = next(it), next(it)
    if has_prev:
        next(it)
    co_ref, o_ref, l_ref = next(it), next(it), next(it)
    dil = SWA_DILATIONS[group]
    w = c_ref.shape[-1]
    pos = lax.broadcasted_iota(jnp.int32, (1, 1, w), 2)
    valid = (pos % dil) == 0
    dist = (w - pos).astype(F32)
    is_newest = pos == w - 1
    heads = range(SWA_HEADS)
    slopes = [2.0 ** (-8.0 * (group * SWA_HEADS + h + 1) / (N_GROUPS * SWA_HEADS)) for h in heads]
    q_cols = [new_ref[:, 0, :, h:h + 1] * (SWA_HEAD_DIM ** -0.5) for h in heads]
    k_cols = [new_ref[:, 1, :, h:h + 1] for h in heads]
    v_cols = [new_ref[:, 2, :, h:h + 1] for h in heads]
    ss, s0s = [], []
    for h in heads:
        k_h = c_ref[:, 0, h]
        co_ref[:, 0, h] = jnp.where(is_newest, k_cols[h], pltpu.roll(k_h, w - 1, axis=2))
        s = jnp.sum(k_h * q_cols[h], axis=1, keepdims=True) - slopes[h] * dist
        ss.append(jnp.where(valid, s, NEG_BIG))
        s0s.append(jnp.sum(k_cols[h] * q_cols[h], axis=1, keepdims=True))
    ms = [jnp.maximum(jnp.max(s, axis=-1, keepdims=True), s0) for s, s0 in zip(ss, s0s)]
    ps = [jnp.exp(s - m) for s, m in zip(ss, ms)]
    p0s = [jnp.exp(s0 - m) for s0, m in zip(s0s, ms)]
    ls = [jnp.sum(p, axis=-1, keepdims=True) + p0 for p, p0 in zip(ps, p0s)]
    accs = []
    for h in heads:
        v_h = c_ref[:, 1, h]
        co_ref[:, 1, h] = jnp.where(is_newest, v_cols[h], pltpu.roll(v_h, w - 1, axis=2))
        accs.append(jnp.sum(v_h * ps[h], axis=-1, keepdims=True))
    outs = [(acc + p0 * v_col) / l for acc, p0, v_col, l in zip(accs, p0s, v_cols, ls)]
    o_ref[...] = jnp.concatenate(outs, axis=-1)
    l_ref[...] = jnp.concatenate([m + jnp.log(l) for m, l in zip(ms, ls)], axis=-1)


def _shift_attend(cache_t, new_t, prev_out, *, layer, group, block_bytes=8 * 1024 * 1024):
    n_att, b = cache_t.shape[:2]
    w = cache_t.shape[-1]
    bb = _pick_tile(b, max(1, block_bytes // (2 * SWA_WIDTH * w * 4)))
    blk = (None, bb, 2, SWA_HEADS, SWA_HEAD_DIM, w)
    in_specs = [pl.BlockSpec(blk, lambda i: (layer, i, 0, 0, 0, 0)),
                pl.BlockSpec((bb, 3, SWA_HEAD_DIM, SWA_HEADS), lambda i: (i, 0, 0, 0))]
    args = [cache_t, new_t]
    aliases = {}
    if prev_out is not None:
        in_specs.append(pl.BlockSpec(memory_space=pl.ANY))
        args.append(prev_out)
        aliases = {2: 0}
    return pl.pallas_call(
        functools.partial(_shift_attend_kernel, group=group, has_prev=prev_out is not None),
        grid=(b // bb,),
        in_specs=in_specs,
        out_specs=[pl.BlockSpec(blk, lambda i: (layer, i, 0, 0, 0, 0)),
                   pl.BlockSpec((bb, SWA_HEAD_DIM, SWA_HEADS), lambda i: (i, 0, 0)),
                   pl.BlockSpec((bb, 1, SWA_HEADS), lambda i: (i, 0, 0))],
        out_shape=[jax.ShapeDtypeStruct(cache_t.shape, cache_t.dtype),
                   jax.ShapeDtypeStruct((b, SWA_HEAD_DIM, SWA_HEADS), F32),
                   jax.ShapeDtypeStruct((b, 1, SWA_HEADS), F32)],
        input_output_aliases=aliases,
        compiler_params=_cparams(("parallel",)),
        name="swa_shift_attend_g%d" % group,
    )(*args)


def _group_merge_kernel(o0, o1, o2, l0, l1, l2, y_ref):
    a0, a1, a2 = l0[...], l1[...], l2[...]
    m = jnp.maximum(jnp.maximum(a0, a1), a2)
    e0, e1, e2 = jnp.exp(a0 - m), jnp.exp(a1 - m), jnp.exp(a2 - m)
    y_ref[...] = (e0 * o0[...] + e1 * o1[...] + e2 * o2[...]) / (e0 + e1 + e2)


def _group_merge(outs, lses):
    return pl.pallas_call(
        _group_merge_kernel,
        out_shape=jax.ShapeDtypeStruct(outs[0].shape, F32),
        name="swa_group_merge",
    )(*outs, *lses)


def _rwkv_proj_kernel(*refs, seq_mode, has_vres, tiles_per_seq):
    it = iter(refs)
    x_ref = next(it)
    prev_ref = next(it)
    vfirst_ref = next(it) if has_vres else None
    g_ref, mix_ref, wr_ref, wk_ref, wv_ref = next(it), next(it), next(it), next(it), next(it)
    w0_ref, w1_ref, w2_ref = next(it), next(it), next(it)
    a0_ref, a1_ref, a2_ref = next(it), next(it), next(it)
    if has_vres:
        v0_ref, v1_ref, v2_ref = next(it), next(it), next(it)
    g1_ref, g2_ref, kk_ref, ka_ref = next(it), next(it), next(it), next(it)
    hsel_ref, hselt_ref = next(it), next(it)
    r_out, ld_out, k_out, v_out, a_out, b_out, g_out, xn_last_out = (next(it) for _ in range(8))

    xn = _rms(x_ref[...], g_ref[...])
    tm = xn.shape[0]
    if seq_mode:
        i = pl.program_id(0)
        pr = _rms(prev_ref[...], g_ref[...])[7:8]
        pr = jnp.where(i % tiles_per_seq == 0, 0.0, pr)
        row = lax.broadcasted_iota(jnp.int32, xn.shape, 0)
        xp = jnp.where(row == 0, pr, pltpu.roll(xn, 1, axis=0))
    else:
        xp = prev_ref[...]
    xx = xp - xn
    mix = mix_ref[...]
    xr, xw, xk, xv, xa, xg = (xn + xx * mix[j:j + 1] for j in range(6))
    r = _bdot(xr, wr_ref[...])
    k = _bdot(xk, wk_ref[...])
    v = _bdot(xv, wv_ref[...])
    z = w0_ref[...] + _bdot(jnp.tanh(_bdot(xw, w1_ref[...])), w2_ref[...])
    w_log = -(jnp.maximum(-z, 0.0) + jnp.log(1.0 + jnp.exp(-jnp.abs(z)))) - 0.5
    ld = -jnp.exp(w_log)
    a = _sigmoid(a0_ref[...] + _bdot(_bdot(xa, a1_ref[...]), a2_ref[...]))
    g = _bdot(_sigmoid(_bdot(xg, g1_ref[...])), g2_ref[...])
    if has_vres:
        v = v + (vfirst_ref[...] - v) * _sigmoid(v0_ref[...] + _bdot(_bdot(xv, v1_ref[...]), v2_ref[...]))
    kk = k * kk_ref[...]
    ssq = jnp.dot((kk * kk).astype(BF16), hsel_ref[...], preferred_element_type=F32)
    rs = lax.rsqrt(jnp.maximum(ssq, 1e-24))
    kk = kk * _dot_sel(rs, hselt_ref[...])
    k = k * (1.0 + (a - 1.0) * ka_ref[...])
    r_out[...] = r
    ld_out[...] = ld
    k_out[...] = k
    v_out[...] = v
    a_out[...] = -kk
    b_out[...] = kk * a
    g_out[...] = g
    xn_last_out[...] = xn[tm - 8:tm] if seq_mode else xn


def _head_selectors(d):
    heads = d // RW_HEAD
    ch = jnp.arange(d) // RW_HEAD
    hsel = (ch[:, None] == jnp.arange(128)[None, :]).astype(BF16)
    return hsel, hsel.T


def _rwkv_proj(x, prev, v_first, p, *, seq_mode, seq, tm_target=256):
    n, d = x.shape
    tm = _pick_tile(seq if seq_mode else n, tm_target)
    has_vres = v_first is not None
    full = lambda a: pl.BlockSpec(a.shape, lambda i: (0,) * a.ndim)
    tile = pl.BlockSpec((tm, d), lambda i: (i, 0))
    args, specs = [x], [tile]
    if seq_mode:
        args.append(x)
        specs.append(pl.BlockSpec((8, d), lambda i: (jnp.maximum(i * (tm // 8) - 1, 0), 0)))
    else:
        args.append(prev)
        specs.append(tile)
    if has_vres:
        args.append(v_first)
        specs.append(tile)
    hsel, hselt = _head_selectors(d)
    names = ["norm_g", "mix", "w_r", "w_k", "w_v", "w0", "w1", "w2", "a0", "a1", "a2"]
    if has_vres:
        names += ["v0", "v1", "v2"]
    names += ["g1", "g2", "k_k", "k_a"]
    consts = [p[name] for name in names] + [(hsel, full(hsel)), (hselt, full(hselt))]
    args += [c[0] for c in consts]
    specs += [c[1] for c in consts]
    outs = pl.pallas_call(
        functools.partial(_rwkv_proj_kernel, seq_mode=seq_mode, has_vres=has_vres,
                          tiles_per_seq=(seq // tm) if seq_mode else 1),
        grid=(n // tm,),
        in_specs=specs,
        out_specs=[tile] * 7 + [pl.BlockSpec((8, d), lambda i: (i, 0)) if seq_mode else tile],
        out_shape=[jax.ShapeDtypeStruct((n, d), F32)] * 7
        + [jax.ShapeDtypeStruct((n // tm * 8 if seq_mode else n, d), F32)],
        compiler_params=_cparams(("parallel",)),
        name="rwkv_proj",
    )(*args)
    return outs


def _wkv_chunk_kernel(r_ref, ld_ref, k_ref, v_ref, a_ref, b_ref, lnw_ref, lnb_ref, rk_ref,
                      y_ref, s_out_ref, s_ref, pre_ref, mat_ref, gam_ref, *, n_chunks):
    C = WKV_CHUNK
    HD = RW_HEAD
    P = 2 * RW_HEAD
    t = pl.program_id(2)
    slot_wr = t % 2
    slot_rd = 1 - slot_wr

    @pl.when(t == 0)
    def _():
        s_ref[...] = jnp.zeros_like(s_ref)
        pre_ref[1] = jnp.zeros(pre_ref.shape[1:], F32)
        mat_ref[1] = jnp.zeros(mat_ref.shape[1:], F32)
        gam_ref[1] = jnp.zeros(gam_ref.shape[1:], F32)

    ri = lax.broadcasted_iota(jnp.int32, (C, C), 0)
    ci = lax.broadcasted_iota(jnp.int32, (C, C), 1)
    tril = ri >= ci
    stril = ri > ci
    first = lambda rows: lax.broadcasted_iota(jnp.int32, (rows, P), 1) < HD
    first_h, first_c, first_2c = first(HD), first(C), first(2 * C)
    row_c = lax.broadcasted_iota(jnp.int32, (C, P), 0)
    lnw, lnb, rk = lnw_ref[...], lnb_ref[...], rk_ref[...]

    def prefix_sum_rows(x):
        shift = 1
        while shift < C:
            x = x + jnp.where(row_c >= shift, pltpu.roll(x, shift, axis=0), 0.0)
            shift *= 2
        return x

    def head_sums(x):
        lo = jnp.sum(jnp.where(first_c, x, 0.0), axis=-1, keepdims=True)
        hi = jnp.sum(jnp.where(first_c, 0.0, x), axis=-1, keepdims=True)
        return jnp.where(first_c, lo, hi)

    chunks = []
    chains = []
    for c in range(n_chunks):
        sl = slice(c * C, (c + 1) * C)
        rc, ldc, kc, vc, ac, bc = (z[sl, :] for z in (r_ref, ld_ref, k_ref, v_ref, a_ref, b_ref))
        cum = prefix_sum_rows(ldc)
        e_pos = jnp.exp(cum)
        e_neg = jnp.exp(-cum)
        at = ac * jnp.exp(cum - ldc)
        rt = rc * e_pos
        bt = bc * e_neg
        kt = kc * e_neg
        last = cum[C - 1:C]
        e_last = jnp.exp(last - cum)
        bh = bc * e_last
        kh = kc * e_last
        gamma = jnp.exp(last)
        pre_ref[slot_wr, c, 2] = head_sums(rc * kc * rk) * vc
        gam_ref[slot_wr, c] = jnp.broadcast_to(gamma, (8, P))
        chunk = dict(at_swapped=pltpu.roll(at, HD, axis=1), rt=rt, v=vc,
                     ar=jnp.concatenate([at, rt], axis=0), bk=jnp.concatenate([bt, kt], axis=0),
                     bkh=jnp.concatenate([bh, kh], axis=0))
        chunks.append(chunk)
        chains += [dict(h=0, chunk=chunk), dict(h=1, chunk=chunk)]

    def stage_scores():
        for ch in chains:
            ck = ch["chunk"]
            mine = first_2c if ch["h"] == 0 else ~first_2c
            sc = _bdot_nt(jnp.where(mine, ck["ar"], 0.0), ck["bk"])
            ch["lp"] = jnp.where(stril, sc[:C, :C], 0.0)
            ch["ak"] = jnp.where(stril, sc[:C, C:], 0.0)
            ch["rbk"] = jnp.concatenate([jnp.where(tril, sc[C:, :C], 0.0),
                                         jnp.where(tril, sc[C:, C:], 0.0)], axis=1)

    def stage_x0():
        for ch in chains:
            ck = ch["chunk"]
            akv = _bdot(ch["ak"], ck["v"])
            ch["x"] = (jnp.where(first_c, akv, ck["at_swapped"]) if ch["h"] == 0
                       else jnp.where(first_c, ck["at_swapped"], akv))

    def stage_level(square):
        nxt = [ch["x"] + _bdot(ch["lp"], ch["x"]) for ch in chains]
        if square:
            for ch in chains:
                ch["lp"] = _bdot(ch["lp"], ch["lp"])
        for ch, x in zip(chains, nxt):
            ch["x"] = x

    def stage_final_dots():
        for ch in chains:
            ck = ch["chunk"]
            vz = jnp.where(first_c, ck["v"], 0.0) if ch["h"] == 0 else jnp.where(first_c, 0.0, ck["v"])
            z = jnp.concatenate([ch["x"], vz], axis=0)
            ch["ry"] = _bdot(ch["rbk"], z)
            ch["qh"] = _bdot_tn(z, ck["bkh"])

    def stage_final_stores():
        for c, ck in enumerate(chunks):
            ry_a, ry_b = chains[2 * c]["ry"], chains[2 * c + 1]["ry"]
            qa, qb = chains[2 * c]["qh"], chains[2 * c + 1]["qh"]
            pre_ref[slot_wr, c, 0] = ck["rt"] + pltpu.roll(jnp.where(first_c, ry_b, ry_a), HD, axis=1)
            pre_ref[slot_wr, c, 1] = jnp.where(first_c, ry_a, ry_b)
            mat_ref[slot_wr, c, 0] = jnp.concatenate([jnp.where(first_h, qa[HD:], 0.0),
                                                      jnp.where(first_h, 0.0, qb[:HD])], axis=0)
            mat_ref[slot_wr, c, 1] = jnp.concatenate([jnp.where(first_h, qa[:HD], 0.0),
                                                      jnp.where(first_h, 0.0, qb[HD:])], axis=0)

    state = [s_ref[...]]

    def state_step(c):
        s0 = state[0]
        y = _bdot_nt(pre_ref[slot_rd, c, 0], s0) + pre_ref[slot_rd, c, 1]
        state[0] = s0 * gam_ref[slot_rd, c, 0:1] + _bdot(s0, mat_ref[slot_rd, c, 0]) + mat_ref[slot_rd, c, 1]
        mu = head_sums(y) * (1.0 / HD)
        yc = y - mu
        var = head_sums(yc * yc) * (1.0 / HD)
        y_ref[c * C:(c + 1) * C, :] = yc * lax.rsqrt(var + RW_LN_EPS) * lnw + lnb + pre_ref[slot_rd, c, 2]

    stages = [stage_scores, stage_x0]
    n = 1
    while n < C:
        stages.append(functools.partial(stage_level, 2 * n < C))
        n *= 2
    stages += [stage_final_dots, stage_final_stores]
    for si, stage in enumerate(stages):
        stage()
        for c in range(si * n_chunks // len(stages), (si + 1) * n_chunks // len(stages)):
            state_step(c)
    s_ref[...] = state[0]

    @pl.when(t == pl.num_programs(2) - 1)
    def _():
        s_out_ref[0] = s_ref[0:HD, 0:HD]
        s_out_ref[1] = s_ref[HD:P, HD:P]


def _wkv_chunked(r, ld, k, v, a, b, ln_w, ln_b, r_k, layer, *, bsz, seq, tb_target=512):
    n, d = r.shape
    heads = d // RW_HEAD
    pairs = heads // 2
    tb = _pick_tile(seq, tb_target)
    per_seq = seq // tb
    n_chunks = tb // WKV_CHUNK
    in_tile = pl.BlockSpec((tb, 128), lambda bi, hp, t: (bi * per_seq + jnp.minimum(t, per_seq - 1), hp))
    out_tile = pl.BlockSpec((tb, 128), lambda bi, hp, t: (bi * per_seq + jnp.maximum(t - 1, 0), hp))
    vec = pl.BlockSpec((None, 1, 128), lambda bi, hp, t: (layer, 0, hp))
    y, st = pl.pallas_call(
        functools.partial(_wkv_chunk_kernel, n_chunks=n_chunks),
        grid=(bsz, pairs, per_seq + 1),
        in_specs=[in_tile] * 6 + [vec] * 3,
        out_specs=[out_tile, pl.BlockSpec((None, 2, RW_HEAD, RW_HEAD), lambda bi, hp, t: (bi, hp, 0, 0))],
        out_shape=[jax.ShapeDtypeStruct((n, d), F32),
                   jax.ShapeDtypeStruct((bsz, heads, RW_HEAD, RW_HEAD), F32)],
        scratch_shapes=[pltpu.VMEM((128, 128), F32),
                        pltpu.VMEM((2, n_chunks, 3, WKV_CHUNK, 128), F32),
                        pltpu.VMEM((2, n_chunks, 2, 128, 128), F32),
                        pltpu.VMEM((2, n_chunks, 8, 128), F32)],
        compiler_params=_cparams(("parallel", "parallel", "arbitrary")),
        name="wkv_chunked",
    )(r, ld, k, v, a, b, ln_w, ln_b, r_k)
    return y, st


def _wkv_step_kernel(*refs, has_prev):
    it = iter(refs)
    s_ref, r_ref, ld_ref, k_ref, a_ref, b_ref, rk_ref, v_ref, lnw_ref, lnb_ref = (next(it) for _ in range(10))
    if has_prev:
        next(it)
    s_out, y_out = next(it), next(it)
    s = s_ref[...]
    r, k = r_ref[...], k_ref[...]
    v = v_ref[...]
    sa = jnp.sum(s * a_ref[...][:, None], axis=2, keepdims=True)
    s_new = s * jnp.exp(ld_ref[...])[:, None] + sa * b_ref[...][:, None] + v * k[:, None]
    y = jnp.sum(s_new * r[:, None], axis=2, keepdims=True)
    mu = jnp.mean(y, axis=1, keepdims=True)
    yc = y - mu
    var = jnp.mean(yc * yc, axis=1, keepdims=True)
    yn = yc * lax.rsqrt(var + RW_LN_EPS) * lnw_ref[...] + lnb_ref[...]
    bonus = jnp.sum(r * k * rk_ref[...], axis=1, keepdims=True)[:, None] * v
    s_out[...] = s_new
    y_out[...] = yn + bonus


def _wkv_step(state_t, prev_out, layer, r, ld, k, v, a, b, ln_w, ln_b, r_k, *, hb=2):
    heads, bsz = state_t.shape[1], state_t.shape[-1]
    d = heads * RW_HEAD
    per_key = lambda z: z.T.reshape(heads, RW_HEAD, bsz)
    per_val = lambda z: z.T.reshape(heads, RW_HEAD, 1, bsz)
    bcast = lambda p, shape: jnp.broadcast_to(p.reshape(shape[:-1] + (1,)), shape)
    st_spec = pl.BlockSpec((None, hb, RW_HEAD, RW_HEAD, bsz), lambda i: (layer, i, 0, 0, 0))
    key_spec = pl.BlockSpec((hb, RW_HEAD, bsz), lambda i: (i, 0, 0))
    val_spec = pl.BlockSpec((hb, RW_HEAD, 1, bsz), lambda i: (i, 0, 0, 0))
    in_specs = [st_spec] + [key_spec] * 6 + [val_spec] * 3
    args = [state_t, per_key(r), per_key(ld), per_key(k), per_key(a), per_key(b),
            bcast(r_k, (heads, RW_HEAD, bsz)), per_val(v),
            bcast(ln_w, (heads, RW_HEAD, 1, bsz)), bcast(ln_b, (heads, RW_HEAD, 1, bsz))]
    aliases = {}
    if prev_out is not None:
        in_specs.append(pl.BlockSpec(memory_space=pl.ANY))
        args.append(prev_out)
        aliases = {10: 0}
    s_new, y = pl.pallas_call(
        functools.partial(_wkv_step_kernel, has_prev=prev_out is not None),
        grid=(heads // hb,),
        in_specs=in_specs,
        out_specs=[st_spec, val_spec],
        out_shape=[jax.ShapeDtypeStruct(state_t.shape, F32),
                   jax.ShapeDtypeStruct((heads, RW_HEAD, 1, bsz), F32)],
        input_output_aliases=aliases,
        compiler_params=_cparams(("parallel",)),
        name="wkv_step",
    )(*args)
    return y.reshape(d, bsz).T, s_new


def _run_layers(x, mem_kv, wkv0, shift0, swa_bufs, W, *, bsz, seq):
    is_prompt = swa_bufs is None
    depth = W["norm_g"].shape[0]
    new_wkv, new_shift = [], []
    new_kv = [[] for _ in range(N_GROUPS)]
    v_first = None
    for l in range(depth):
        ng = lambda j: _param(W["norm_g"], l, j)
        x = _ffn(x, ng(0), W["ffn_w_in"], W["ffn_w_out"], l, 0)
        i = l // 2
        if l % 2 == 0:
            at_i = lambda name, *more: _param(W[name], i, *more)
            p = dict(norm_g=ng(1), mix=at_i("rw_mix"), w_r=at_i("rw_w_rkv", 0), w_k=at_i("rw_w_rkv", 1),
                     w_v=at_i("rw_w_rkv", 2), w0=at_i("rw_w0"), w1=at_i("rw_w1"), w2=at_i("rw_w2"),
                     a0=at_i("rw_a0"), a1=at_i("rw_a1"), a2=at_i("rw_a2"), g1=at_i("rw_g1"), g2=at_i("rw_g2"),
                     k_k=at_i("rw_k_k"), k_a=at_i("rw_k_a"))
            if i > 0:
                p.update(v0=_param(W["rw_v0"], i - 1), v1=_param(W["rw_v1"], i - 1), v2=_param(W["rw_v2"], i - 1))
            prev = None if is_prompt else shift0[i]
            r, ld, k, v, a, b, g, xn_last = _rwkv_proj(x, prev, v_first if i > 0 else None, p,
                                                      seq_mode=is_prompt, seq=seq)
            if i == 0:
                v_first = v
            if is_prompt:
                y, s_new = _wkv_chunked(r, ld, k, v, a, b, W["rw_ln_w"], W["rw_ln_b"], W["rw_r_k"], i,
                                        bsz=bsz, seq=seq)
                tiles = xn_last.shape[0] // 8 // bsz
                shift_new = xn_last.reshape(bsz, tiles, 8, -1)[:, -1, -1]
                new_wkv.append(s_new)
            else:
                y, s_new = _wkv_step(wkv0, new_wkv[0] if new_wkv else None, i, r, ld, k, v, a, b,
                                     W["rw_ln_w"][i, 0], W["rw_ln_b"][i, 0], W["rw_r_k"][i, 0])
                shift_new = xn_last
                new_wkv = [s_new]
            new_shift.append(shift_new)
            x = _linear(y, W["rw_w_o"], i, mul=g, res=x, name="rwkv_out")
        else:
            outs, lses = [], []
            if is_prompt:
                qkv_groups = _swa_qkv_prompt(x, ng(1), W["swa_w_qkv"], i, bsz=bsz, seq=seq)
                for gi in range(N_GROUPS):
                    o, lse = _band_attention(qkv_groups[gi], gi)
                    outs.append(o)
                    lses.append(lse)
                    dil = SWA_DILATIONS[gi]
                    keep = min(SWA_WINDOWS[gi], seq)
                    kv_tail = qkv_groups[gi][:, :, (seq - keep) // dil:, SWA_WIDTH:]
                    new_kv[gi].append(jnp.swapaxes(kv_tail, 1, 2).reshape(
                        bsz, keep, 2, SWA_HEADS, SWA_HEAD_DIM))
                x = _swa_merge(outs, lses, x, _param(W["swa_w_o"], i), seq=seq)
            else:
                qkv = _linear(x, W["swa_w_qkv"], i, norm_g=ng(1), tn_target=512, name="swa_qkv")
                qkv5 = qkv.reshape(bsz, N_GROUPS, 3, SWA_HEADS, SWA_HEAD_DIM)
                for gi in range(N_GROUPS):
                    prev_out = new_kv[gi][0] if new_kv[gi] else None
                    new_t = jnp.swapaxes(qkv5[:, gi], -1, -2)
                    cache_out, o, lse = _shift_attend(swa_bufs[gi], new_t, prev_out, layer=i, group=gi)
                    new_kv[gi] = [cache_out]
                    outs.append(o)
                    lses.append(lse)
                y = jnp.swapaxes(_group_merge(outs, lses), 1, 2).reshape(bsz, SWA_WIDTH)
                x = _linear(y, W["swa_w_o"], i, res=x, name="swa_out")
        if is_prompt:
            x = _xa_prompt(x, ng(2), _param(W["xa_w_q"], l), mem_kv, _param(W["xa_w_o"], l), l, seq=seq)
        else:
            q = _linear(x, W["xa_w_q"], l, norm_g=ng(2), name="xa_q")
            o = _xa_sample(q, mem_kv, l)
            x = _linear(o, W["xa_w_o"], l, res=x, name="xa_out")
        x = _ffn(x, ng(3), W["ffn_w_in"], W["ffn_w_out"], l, 1,
                 final_g=_param(W["final_norm_g"], 0) if l == depth - 1 else None)
    return x, new_wkv, new_shift, new_kv


def kernel(x_prompt, x_sample, cache_mem_kv, state_rwkv_wkv, state_rwkv_shift, cache_swa_kv_g0, cache_swa_kv_g1, cache_swa_kv_g2, mem_prompt, norm_g, mem_norm_g, final_norm_g, ffn_w_in, ffn_w_out, rw_mix, rw_w_rkv, rw_w_o, rw_w0, rw_w1, rw_w2, rw_a0, rw_a1, rw_a2, rw_v0, rw_v1, rw_v2, rw_g1, rw_g2, rw_k_k, rw_k_a, rw_r_k, rw_ln_w, rw_ln_b, swa_w_qkv, swa_w_o, xa_w_q, xa_w_kv, xa_w_o):
    bf = lambda w: w.astype(BF16)
    vec = lambda a: a.reshape(a.shape[:-1] + (1, a.shape[-1]))
    depth = norm_g.shape[0]
    bsz, seq, d = x_prompt.shape
    n_mem = mem_prompt.shape[1]
    W = dict(norm_g=vec(norm_g), final_norm_g=final_norm_g.reshape(1, 1, d),
             ffn_w_in=bf(ffn_w_in), ffn_w_out=bf(ffn_w_out),
             rw_mix=rw_mix, rw_w_rkv=bf(rw_w_rkv), rw_w_o=bf(rw_w_o), rw_w0=vec(rw_w0), rw_w1=bf(rw_w1),
             rw_w2=bf(rw_w2), rw_a0=vec(rw_a0), rw_a1=bf(rw_a1), rw_a2=bf(rw_a2), rw_v0=vec(rw_v0),
             rw_v1=bf(rw_v1), rw_v2=bf(rw_v2), rw_g1=bf(rw_g1), rw_g2=bf(rw_g2), rw_k_k=vec(rw_k_k),
             rw_k_a=vec(rw_k_a), rw_r_k=rw_r_k.reshape(-1, 1, d), rw_ln_w=vec(rw_ln_w), rw_ln_b=vec(rw_ln_b),
             swa_w_qkv=bf(swa_w_qkv), swa_w_o=bf(swa_w_o), xa_w_q=bf(xa_w_q), xa_w_o=bf(xa_w_o))

    mem2 = mem_prompt.reshape(bsz * n_mem, d)
    xa_w_kv_b = bf(xa_w_kv)
    mem_g = vec(mem_norm_g)
    mem_kv_p = jnp.stack([_linear(mem2, xa_w_kv_b, l, norm_g=_param(mem_g, l), name="mem_kv")
                          for l in range(depth)])
    y_p, wkv_p, shift_p, kv_p = _run_layers(
        x_prompt.reshape(bsz * seq, d), mem_kv_p.reshape(depth, bsz, n_mem, 2 * XA_WIDTH),
        None, None, None, W, bsz=bsz, seq=seq)

    dbsz, dseq, _ = x_sample.shape
    assert dseq == 1
    caches = tuple(jnp.transpose(c, (0, 1, 3, 4, 5, 2))
                   for c in (cache_swa_kv_g0, cache_swa_kv_g1, cache_swa_kv_g2))
    wkv_t = jnp.transpose(state_rwkv_wkv, (0, 2, 3, 4, 1))
    y_s, wkv_s, shift_s, kv_s = _run_layers(
        x_sample.reshape(dbsz, d), cache_mem_kv,
        wkv_t, state_rwkv_shift, caches, W, bsz=dbsz, seq=1)
    swa_s = tuple(jnp.transpose(kv_s[g][0], (0, 1, 5, 2, 3, 4)) for g in range(N_GROUPS))
    wkv_s = jnp.transpose(wkv_s[0], (0, 4, 1, 2, 3))

    return (y_p.reshape(bsz, seq, d), y_s.reshape(dbsz, 1, d),
            mem_kv_p.reshape(depth, bsz, n_mem, 2, XA_HEADS, XA_HEAD_DIM),
            jnp.stack(wkv_p), jnp.stack(shift_p),
            jnp.stack(kv_p[0]), jnp.stack(kv_p[1]), jnp.stack(kv_p[2]),
            wkv_s, jnp.stack(shift_s), swa_s[0], swa_s[1], swa_s[2])
```
